```python
import jax, jax.numpy as jnp
from jax import lax
import numpy as np

D_MODEL = 1024
BATCH = 8
SEQ = 4096
DEPTH = 1

CHUNK = 64
RET_WIDTH = D_MODEL // 2
RET_HEAD_DIM = 128
RET_HEADS = RET_WIDTH // RET_HEAD_DIM
GMLP_WIDTH = D_MODEL - RET_WIDTH
GMLP_GROUP_DIM = 128
GMLP_GROUPS = GMLP_WIDTH // GMLP_GROUP_DIM
GMLP_BLOCK = 128
MIX_WIDTH = RET_WIDTH + GMLP_WIDTH
PROJ_WIDTH = 4 * RET_WIDTH + 2 * GMLP_WIDTH
D_FF = 256 * ((8 * D_MODEL + 3 * 256 - 1) // (3 * 256))
ROPE_THETA = 10000.0
EPS = 1e-6

kernel_name = "hybrid_retention_gmlp_block"


def rmsnorm(x, g):
    xf = x.astype(jnp.float32)
    y = xf * lax.rsqrt(jnp.mean(xf * xf, axis=-1, keepdims=True) + EPS)
    return (y * g.astype(jnp.float32)).astype(x.dtype)


def layernorm(x, g, b):
    xf = x.astype(jnp.float32)
    mu = jnp.mean(xf, axis=-1, keepdims=True)
    var = jnp.mean(jnp.square(xf - mu), axis=-1, keepdims=True)
    y = (xf - mu) * lax.rsqrt(var + EPS)
    return (y * g.astype(jnp.float32) + b.astype(jnp.float32)).astype(x.dtype)


def head_groupnorm(y, g):
    B, S, H, D = y.shape
    yf = y.astype(jnp.float32)
    mu = jnp.mean(yf, axis=-1, keepdims=True)
    var = jnp.mean(jnp.square(yf - mu), axis=-1, keepdims=True)
    yn = ((yf - mu) * lax.rsqrt(var + EPS)).reshape(B, S, H * D)
    return (yn * g.astype(jnp.float32)).astype(y.dtype)


def rope(x, pos):
    D = x.shape[-1]
    inv = ROPE_THETA ** (-jnp.arange(0, D, 2, dtype=jnp.float32) / D)
    ang = pos.astype(jnp.float32)[:, None] * inv[None, :]
    cos = jnp.cos(ang).astype(x.dtype)[None, :, None, :]
    sin = jnp.sin(ang).astype(x.dtype)[None, :, None, :]
    x1, x2 = x[..., : D // 2], x[..., D // 2:]
    return jnp.concatenate([x1 * cos - x2 * sin, x2 * cos + x1 * sin], axis=-1)


def retention(q, k, v):
    B, S, H, D = q.shape
    N = S // CHUNK
    dt = q.dtype
    log_gamma = jnp.log(1.0 - jnp.power(2.0, -5.0 - jnp.arange(H, dtype=jnp.float32)))
    idx = jnp.arange(CHUNK, dtype=jnp.float32)
    inner_decay = jnp.exp(log_gamma[:, None, None] * jnp.abs(idx[:, None] - idx[None, :])).astype(dt)
    k_decay = jnp.exp(log_gamma[:, None] * (CHUNK - 1 - idx)[None, :]).astype(dt)
    q_decay = jnp.exp(log_gamma[:, None] * (idx + 1)[None, :]).astype(dt)
    chunk_decay = jnp.exp(log_gamma * CHUNK).astype(dt)

    qc = (q * (D ** -0.5)).reshape(B, N, CHUNK, H, D)
    kc = k.reshape(B, N, CHUNK, H, D)
    vc = v.reshape(B, N, CHUNK, H, D)

    scores = jnp.einsum('bnihd,bnjhd->bnhij', qc, kc) * inner_decay[None, None]
    inner = jnp.einsum('bnhij,bnjhe->bnihe', scores, vc)

    kv = jnp.einsum('bnjhd,bnjhe,hj->nbhde', kc, vc, k_decay)

    def step(state, u):
        return chunk_decay[:, None, None] * state + u, state

    _, prev = lax.scan(step, jnp.zeros(kv.shape[1:], kv.dtype), kv)
    cross = jnp.einsum('bnihd,nbhde,hi->bnihe', qc, prev, q_decay)
    return (inner + cross).reshape(B, S, H, D)


def spatial_gating(u, v, w_s, b_s, ln_g, ln_b):
    B, S, _ = v.shape
    M = S // GMLP_BLOCK
    v = layernorm(v, ln_g, ln_b)
    vb = v.reshape(B, M, GMLP_BLOCK, GMLP_GROUPS, GMLP_GROUP_DIM)
    cpos = jnp.arange(GMLP_BLOCK) // CHUNK
    mask = cpos[:, None] >= cpos[None, :]
    w = jnp.where(mask[None], w_s, jnp.zeros((), w_s.dtype))
    mixed = jnp.einsum('gts,bmsgc->bmtgc', w, vb) + b_s.T[None, None, :, :, None]
    return u * mixed.reshape(B, S, GMLP_WIDTH)


def setup_inputs(seed: int = 0) -> dict:
    key = jax.random.key(seed)
    ks = jax.random.split(key, 16)
    f32 = jnp.float32
    nrm = lambda k, shape, s: jax.random.normal(k, shape, f32) * s
    return {
        "x": jax.random.normal(ks[0], (BATCH, SEQ, D_MODEL), f32),
        "norm1_g": 1.0 + nrm(ks[1], (DEPTH, D_MODEL), 0.05),
        "w_in": nrm(ks[2], (DEPTH, D_MODEL, PROJ_WIDTH), D_MODEL ** -0.5),
        "ret_gn_g": 1.0 + nrm(ks[3], (DEPTH, RET_WIDTH), 0.05),
        "gmlp_ln_g": 1.0 + nrm(ks[4], (DEPTH, GMLP_WIDTH), 0.05),
        "gmlp_ln_b": nrm(ks[5], (DEPTH, GMLP_WIDTH), 0.02),
        "w_s": nrm(ks[6], (DEPTH, GMLP_GROUPS, GMLP_BLOCK, GMLP_BLOCK), GMLP_BLOCK ** -0.5),
        "b_s": 1.0 + nrm(ks[7], (DEPTH, GMLP_GROUPS, GMLP_BLOCK), 0.1),
        "w_out": nrm(ks[8], (DEPTH, MIX_WIDTH, D_MODEL), MIX_WIDTH ** -0.5),
        "norm2_g": 1.0 + nrm(ks[9], (DEPTH, D_MODEL), 0.05),
        "w_ffn_gate": nrm(ks[10], (DEPTH, D_MODEL, D_FF), D_MODEL ** -0.5),
        "w_ffn_up": nrm(ks[11], (DEPTH, D_MODEL, D_FF), D_MODEL ** -0.5),
        "w_ffn_down": nrm(ks[12], (DEPTH, D_FF, D_MODEL), D_FF ** -0.5),
        "final_g": 1.0 + nrm(ks[13], (D_MODEL,), 0.05),
    }


def reference(x, norm1_g, w_in, ret_gn_g, gmlp_ln_g, gmlp_ln_b, w_s, b_s, w_out,
              norm2_g, w_ffn_gate, w_ffn_up, w_ffn_down, final_g):
    B, S, _ = x.shape
    pos = jnp.arange(S)
    splits = [RET_WIDTH, 2 * RET_WIDTH, 3 * RET_WIDTH, 4 * RET_WIDTH, 4 * RET_WIDTH + GMLP_WIDTH]
    for l in range(DEPTH):
        h = rmsnorm(x, norm1_g[l])
        p = h @ w_in[l]
        q, k, v, g, u, vg = jnp.split(p, splits, axis=-1)
        q = rope(q.reshape(B, S, RET_HEADS, RET_HEAD_DIM), pos)
        k = rope(k.reshape(B, S, RET_HEADS, RET_HEAD_DIM), pos)
        v = v.reshape(B, S, RET_HEADS, RET_HEAD_DIM)
        ret = head_groupnorm(retention(q, k, v), ret_gn_g[l]) * jax.nn.silu(g)
        gm = spatial_gating(jax.nn.gelu(u), jax.nn.gelu(vg), w_s[l], b_s[l],
                            gmlp_ln_g[l], gmlp_ln_b[l])
        x = x + jnp.concatenate([ret, gm], axis=-1) @ w_out[l]
        h2 = rmsnorm(x, norm2_g[l])
        x = x + (jax.nn.silu(h2 @ w_ffn_gate[l]) * (h2 @ w_ffn_up[l])) @ w_ffn_down[l]
    return rmsnorm(x, final_g)
```

```python
import functools

import numpy as np
import jax
import jax.numpy as jnp
from jax import lax
from jax.experimental import pallas as pl
from jax.experimental.pallas import tpu as pltpu

D_MODEL = 1024
CHUNK = 64
RET_WIDTH = D_MODEL // 2
HEAD_DIM = 128
HEADS = RET_WIDTH // HEAD_DIM
GMLP_WIDTH = D_MODEL - RET_WIDTH
GROUP_DIM = 128
GROUPS = GMLP_WIDTH // GROUP_DIM
GMLP_BLOCK = 128
D_FF = 2816
ROPE_THETA = 10000.0
EPS = 1e-6

LANES = 128
V7X_VMEM_LIMIT_BYTES = 56 * 1024 * 1024

SEQ_TILE = 256
FFN_TILE = 256
FF_CHUNK = 256

F32 = jnp.float32
BF16 = jnp.bfloat16


def _rms_scale(x):
    return x * lax.rsqrt(jnp.mean(x * x, axis=-1, keepdims=True) + EPS)


def _dot(a, b):
    return jnp.dot(a, b, preferred_element_type=F32)


def _dot_nt(a, b):
    return lax.dot_general(a, b, (((1,), (1,)), ((), ())), preferred_element_type=F32)


def _dot_tn(a, b):
    return lax.dot_general(a, b, (((0,), (0,)), ((), ())), preferred_element_type=F32)


def _mixer_kernel(x_ref, g1_ref, w_in_ref, cos_ref, sin_ref, dmask_ref, qdec_ref, kdec_ref,
                  gn_g_ref, ln_g_ref, ln_b_ref, ws_ref, bs_ref, w_out_ref,
                  o_ref, state_ref, *, tile_decay):
    @pl.when(pl.program_id(1) == 0)
    def _():
        state_ref[...] = jnp.zeros_like(state_ref)

    x = x_ref[0]
    tile = x.shape[0]
    h = (_rms_scale(x) * g1_ref[...]).astype(BF16)

    def proj(i):
        return _dot(h, w_in_ref[:, i * RET_WIDTH:(i + 1) * RET_WIDTH])

    q, k, v, g, u, vg = (proj(i) for i in range(6))
    cos = cos_ref[...]
    sin = sin_ref[...]

    ret_heads = []
    for hd in range(HEADS):
        sl = slice(hd * HEAD_DIM, (hd + 1) * HEAD_DIM)
        qh, kh, vh = q[:, sl], k[:, sl], v[:, sl]
        qh = (qh * cos + pltpu.roll(qh, HEAD_DIM // 2, 1) * sin).astype(BF16)
        kh = (kh * cos + pltpu.roll(kh, HEAD_DIM // 2, 1) * sin).astype(BF16)
        scores = (_dot_nt(qh, kh) * dmask_ref[hd]).astype(BF16)
        inner = _dot(scores, vh.astype(BF16))
        state = state_ref[hd]
        cross = _dot(qh, state.astype(BF16)) * qdec_ref[hd]
        kv = _dot_tn(kh, (vh * kdec_ref[hd]).astype(BF16))
        state_ref[hd] = tile_decay[hd] * state + kv
        y = inner + cross
        mu = jnp.mean(y, axis=-1, keepdims=True)
        yc = y - mu
        var = jnp.mean(yc * yc, axis=-1, keepdims=True)
        ret_heads.append(yc * lax.rsqrt(var + EPS))
    ret = jnp.concatenate(ret_heads, axis=-1) * gn_g_ref[...] * (g * jax.nn.sigmoid(g))

    ug = jax.nn.gelu(u)
    vgg = jax.nn.gelu(vg)
    mu = jnp.mean(vgg, axis=-1, keepdims=True)
    vc = vgg - mu
    var = jnp.mean(vc * vc, axis=-1, keepdims=True)
    vn = (vc * lax.rsqrt(var + EPS) * ln_g_ref[...] + ln_b_ref[...]).astype(BF16)
    row = lax.broadcasted_iota(jnp.int32, (GMLP_BLOCK, GMLP_BLOCK), 0) // CHUNK
    col = lax.broadcasted_iota(jnp.int32, (GMLP_BLOCK, GMLP_BLOCK), 1) // CHUNK
    causal = row >= col
    mixed_blocks = []
    for m in range(tile // GMLP_BLOCK):
        rows = slice(m * GMLP_BLOCK, (m + 1) * GMLP_BLOCK)
        groups = []
        for gi in range(GROUPS):
            w = jnp.where(causal, ws_ref[gi], 0.0).astype(BF16)
            groups.append(_dot(w, vn[rows, gi * GROUP_DIM:(gi + 1) * GROUP_DIM]))
        mixed_blocks.append(jnp.concatenate(groups, axis=-1) + bs_ref[...])
    gm = ug * jnp.concatenate(mixed_blocks, axis=0)

    mix = jnp.concatenate([ret, gm], axis=-1).astype(BF16)
    o_ref[0] = x + _dot(mix, w_out_ref[...])


def _ffn_kernel(x_ref, g2_ref, wg_ref, wu_ref, wd_ref, gf_ref, o_ref):
    x = x_ref[...]
    h = (_rms_scale(x) * g2_ref[...]).astype(BF16)
    acc = x
    for c in range(D_FF // FF_CHUNK):
        sl = slice(c * FF_CHUNK, (c + 1) * FF_CHUNK)
        gate = _dot(h, wg_ref[:, sl])
        up = _dot(h, wu_ref[:, sl])
        act = (gate * jax.nn.sigmoid(gate) * up).astype(BF16)
        acc = acc + _dot(act, wd_ref[sl, :])
    o_ref[...] = _rms_scale(acc) * gf_ref[...]


def _const_spec(shape):
    zeros = (0,) * len(shape)
    return pl.BlockSpec(shape, lambda *_: zeros, pipeline_mode=pl.Buffered(1))


def _decay_tables(tile):
    log_gamma = np.log(1.0 - np.power(2.0, -5.0 - np.arange(HEADS, dtype=np.float64)))
    idx = np.arange(tile, dtype=np.float64)
    diff = idx[:, None] - idx[None, :]
    same = (np.arange(tile)[:, None] // CHUNK) == (np.arange(tile)[None, :] // CHUNK)
    earlier = (np.arange(tile)[:, None] // CHUNK) > (np.arange(tile)[None, :] // CHUNK)
    expo = np.where(same, np.abs(diff), diff)
    scale = HEAD_DIM ** -0.5
    dmask = np.where(same | earlier, np.exp(log_gamma[:, None, None] * expo[None]), 0.0) * scale
    qdec = np.exp(log_gamma[:, None] * (idx + 1.0)[None, :]) * scale
    kdec = np.exp(log_gamma[:, None] * (tile - 1.0 - idx)[None, :])
    tile_decay = tuple(float(v) for v in np.exp(log_gamma * tile))
    bcast = lambda a: np.broadcast_to(a[:, :, None], (HEADS, tile, LANES))
    return (jnp.asarray(dmask, F32), jnp.asarray(bcast(qdec), F32),
            jnp.asarray(bcast(kdec), F32), tile_decay)


def _mixer(x, norm1_g, w_in, ret_gn_g, gmlp_ln_g, gmlp_ln_b, w_s, b_s, w_out):
    B, S, D = x.shape
    T = SEQ_TILE
    pos = jnp.arange(S, dtype=F32)
    inv = ROPE_THETA ** (-jnp.arange(0, HEAD_DIM, 2, dtype=F32) / HEAD_DIM)
    ang = pos[:, None] * inv[None, :]
    cos = jnp.concatenate([jnp.cos(ang), jnp.cos(ang)], axis=-1)
    sin = jnp.concatenate([-jnp.sin(ang), jnp.sin(ang)], axis=-1)
    dmask, qdec, kdec, tile_decay = _decay_tables(T)
    bias = jnp.repeat(b_s.T, GROUP_DIM, axis=1)

    row = lambda a: a.reshape(1, -1)
    in_specs = [
        pl.BlockSpec((1, T, D), lambda b, t: (b, t, 0)),
        _const_spec((1, D)),
        _const_spec(w_in.shape),
        pl.BlockSpec((T, LANES), lambda b, t: (t, 0)),
        pl.BlockSpec((T, LANES), lambda b, t: (t, 0)),
        _const_spec(dmask.shape),
        _const_spec(qdec.shape),
        _const_spec(kdec.shape),
        _const_spec((1, RET_WIDTH)),
        _const_spec((1, GMLP_WIDTH)),
        _const_spec((1, GMLP_WIDTH)),
        _const_spec(w_s.shape),
        _const_spec(bias.shape),
        _const_spec(w_out.shape),
    ]
    return pl.pallas_call(
        functools.partial(_mixer_kernel, tile_decay=tile_decay),
        grid=(B, S // T),
        in_specs=in_specs,
        out_specs=pl.BlockSpec((1, T, D), lambda b, t: (b, t, 0)),
        out_shape=jax.ShapeDtypeStruct((B, S, D), F32),
        scratch_shapes=[pltpu.VMEM((HEADS, HEAD_DIM, HEAD_DIM), F32)],
        compiler_params=pltpu.CompilerParams(
            dimension_semantics=("arbitrary", "arbitrary"),
            vmem_limit_bytes=V7X_VMEM_LIMIT_BYTES),
        name="token_mixer",
    )(x, row(norm1_g), w_in.astype(BF16), cos, sin, dmask, qdec, kdec,
      row(ret_gn_g), row(gmlp_ln_g), row(gmlp_ln_b), w_s, bias, w_out.astype(BF16))


def _ffn(x, norm2_g, w_gate, w_up, w_down, final_g):
    N, D = x.shape
    T = FFN_TILE
    row = lambda a: a.reshape(1, -1)
    return pl.pallas_call(
        _ffn_kernel,
        grid=(N // T,),
        in_specs=[
            pl.BlockSpec((T, D), lambda i: (i, 0)),
            _const_spec((1, D)),
            _const_spec(w_gate.shape),
            _const_spec(w_up.shape),
            _const_spec(w_down.shape),
            _const_spec((1, D)),
        ],
        out_specs=pl.BlockSpec((T, D), lambda i: (i, 0)),
        out_shape=jax.ShapeDtypeStruct((N, D), F32),
        compiler_params=pltpu.CompilerParams(
            dimension_semantics=("arbitrary",),
            vmem_limit_bytes=V7X_VMEM_LIMIT_BYTES),
        name="swiglu_ffn",
    )(x, row(norm2_g), w_gate.astype(BF16), w_up.astype(BF16), w_down.astype(BF16), row(final_g))


def kernel(x, norm1_g, w_in, ret_gn_g, gmlp_ln_g, gmlp_ln_b, w_s, b_s, w_out, norm2_g,
           w_ffn_gate, w_ffn_up, w_ffn_down, final_g):
    B, S, D = x.shape
    assert w_in.shape[0] == 1, "the FFN kernel fuses the final norm, so exactly one layer is supported"
    assert D == D_MODEL and S % SEQ_TILE == 0 and (B * S) % FFN_TILE == 0
    x1 = _mixer(x, norm1_g[0], w_in[0], ret_gn_g[0], gmlp_ln_g[0], gmlp_ln_b[0],
                w_s[0], b_s[0], w_out[0])
    out = _ffn(x1.reshape(B * S, D), norm2_g[0], w_ffn_gate[0], w_ffn_up[0], w_ffn_down[0], final_g)
    return out.reshape(B, S, D)
```

```python
import functools

import numpy as np
import jax
import jax.numpy as jnp
from jax import lax
from jax.experimental import pallas as pl
from jax.experimental.pallas import tpu as pltpu

D_MODEL = 1024
CHUNK = 64
RET_WIDTH = D_MODEL // 2
HEAD_DIM = 128
HEADS = RET_WIDTH // HEAD_DIM
GMLP_WIDTH = D_MODEL - RET_WIDTH
GROUP_DIM = 128
GROUPS = GMLP_WIDTH // GROUP_DIM
GMLP_BLOCK = 128
D_FF = 2816
ROPE_THETA = 10000.0
EPS = 1e-6

LANES = 128
V7X_VMEM_LIMIT_BYTES = 56 * 1024 * 1024

SEQ_TILE = 256
FFN_TILE = 1024
FFN_SUB = 256
FF_CHUNK = 256

F32 = jnp.float32
BF16 = jnp.bfloat16


def _rms_scale(x):
    return x * lax.rsqrt(jnp.mean(x * x, axis=-1, keepdims=True) + EPS)


def _dot(a, b):
    return jnp.dot(a, b, preferred_element_type=F32)


def _dot_nt(a, b):
    return lax.dot_general(a, b, (((1,), (1,)), ((), ())), preferred_element_type=F32)


def _dot_tn(a, b):
    return lax.dot_general(a, b, (((0,), (0,)), ((), ())), preferred_element_type=F32)


def _mixer_kernel(x_ref, g1_ref, w_in_ref, cos_ref, sin_ref, dmask_ref, qdec_ref, kdec_ref,
                  gn_g_ref, ln_g_ref, ln_b_ref, ws_ref, bs_ref, w_out_ref,
                  o_ref, state_ref, *, tile_decay):
    @pl.when(pl.program_id(1) == 0)
    def _():
        state_ref[...] = jnp.zeros_like(state_ref)

    x = x_ref[0]
    tile = x.shape[0]
    h = (_rms_scale(x) * g1_ref[...]).astype(BF16)

    def proj(i):
        return _dot(h, w_in_ref[:, i * RET_WIDTH:(i + 1) * RET_WIDTH])

    q, k, v, g, u, vg = (proj(i) for i in range(6))
    cos = cos_ref[...]
    sin = sin_ref[...]

    ret_heads = []
    for hd in range(HEADS):
        sl = slice(hd * HEAD_DIM, (hd + 1) * HEAD_DIM)
        qh, kh, vh = q[:, sl], k[:, sl], v[:, sl]
        qh = (qh * cos + pltpu.roll(qh, HEAD_DIM // 2, 1) * sin).astype(BF16)
        kh = (kh * cos + pltpu.roll(kh, HEAD_DIM // 2, 1) * sin).astype(BF16)
        scores = (_dot_nt(qh, kh) * dmask_ref[hd]).astype(BF16)
        inner = _dot(scores, vh.astype(BF16))
        state = state_ref[hd]
        cross = _dot(qh, state.astype(BF16)) * qdec_ref[hd]
        kv = _dot_tn(kh, (vh * kdec_ref[hd]).astype(BF16))
        state_ref[hd] = tile_decay[hd] * state + kv
        y = inner + cross
        mu = jnp.mean(y, axis=-1, keepdims=True)
        yc = y - mu
        var = jnp.mean(yc * yc, axis=-1, keepdims=True)
        ret_heads.append(yc * lax.rsqrt(var + EPS))
    ret = jnp.concatenate(ret_heads, axis=-1) * gn_g_ref[...] * (g * jax.nn.sigmoid(g))

    ug = jax.nn.gelu(u)
    vgg = jax.nn.gelu(vg)
    mu = jnp.mean(vgg, axis=-1, keepdims=True)
    vc = vgg - mu
    var = jnp.mean(vc * vc, axis=-1, keepdims=True)
    vn = (vc * lax.rsqrt(var + EPS) * ln_g_ref[...] + ln_b_ref[...]).astype(BF16)
    row = lax.broadcasted_iota(jnp.int32, (GMLP_BLOCK, GMLP_BLOCK), 0) // CHUNK
    col = lax.broadcasted_iota(jnp.int32, (GMLP_BLOCK, GMLP_BLOCK), 1) // CHUNK
    causal = row >= col
    mixed_blocks = []
    for m in range(tile // GMLP_BLOCK):
        rows = slice(m * GMLP_BLOCK, (m + 1) * GMLP_BLOCK)
        groups = []
        for gi in range(GROUPS):
            w = jnp.where(causal, ws_ref[gi], 0.0).astype(BF16)
            groups.append(_dot(w, vn[rows, gi * GROUP_DIM:(gi + 1) * GROUP_DIM]))
        mixed_blocks.append(jnp.concatenate(groups, axis=-1) + bs_ref[...])
    gm = ug * jnp.concatenate(mixed_blocks, axis=0)

    mix = jnp.concatenate([ret, gm], axis=-1).astype(BF16)
    o_ref[0] = x + _dot(mix, w_out_ref[...])


def _ffn_kernel(x_ref, g2_ref, wg_ref, wu_ref, wd_ref, gf_ref, o_ref):
    n_chunks = D_FF // FF_CHUNK
    n_sub = x_ref.shape[0] // FFN_SUB

    def rows(s):
        return pl.ds(s * FFN_SUB, FFN_SUB)

    def prologue(s):
        return (_rms_scale(x_ref[rows(s), :]) * g2_ref[...]).astype(BF16)

    def epilogue(s, acc):
        o_ref[rows(s), :] = _rms_scale(x_ref[rows(s), :] + acc) * gf_ref[...]

    def gate_up(h, c):
        sl = slice(c * FF_CHUNK, (c + 1) * FF_CHUNK)
        return _dot(h, wg_ref[:, sl]), _dot(h, wu_ref[:, sl])

    h = prologue(0)
    done = None
    for s in range(n_sub):
        h_next = None
        acc = None
        gate, up = gate_up(h, 0)
        for c in range(n_chunks):
            act = (gate * jax.nn.sigmoid(gate) * up).astype(BF16)
            if c + 1 < n_chunks:
                gate, up = gate_up(h, c + 1)
            part = _dot(act, wd_ref[c * FF_CHUNK:(c + 1) * FF_CHUNK, :])
            acc = part if acc is None else acc + part
            if c == 0 and s + 1 < n_sub:
                h_next = prologue(s + 1)
            if c == 1 and done is not None:
                epilogue(*done)
        done = (s, acc)
        h = h_next
    epilogue(*done)


def _const_spec(shape):
    zeros = (0,) * len(shape)
    return pl.BlockSpec(shape, lambda *_: zeros, pipeline_mode=pl.Buffered(1))


def _decay_tables(tile):
    log_gamma = np.log(1.0 - np.power(2.0, -5.0 - np.arange(HEADS, dtype=np.float64)))
    idx = np.arange(tile, dtype=np.float64)
    diff = idx[:, None] - idx[None, :]
    same = (np.arange(tile)[:, None] // CHUNK) == (np.arange(tile)[None, :] // CHUNK)
    earlier = (np.arange(tile)[:, None] // CHUNK) > (np.arange(tile)[None, :] // CHUNK)
    expo = np.where(same, np.abs(diff), diff)
    scale = HEAD_DIM ** -0.5
    dmask = np.where(same | earlier, np.exp(log_gamma[:, None, None] * expo[None]), 0.0) * scale
    qdec = np.exp(log_gamma[:, None] * (idx + 1.0)[None, :]) * scale
    kdec = np.exp(log_gamma[:, None] * (tile - 1.0 - idx)[None, :])
    tile_decay = tuple(float(v) for v in np.exp(log_gamma * tile))
    bcast = lambda a: np.broadcast_to(a[:, :, None], (HEADS, tile, LANES))
    return (jnp.asarray(dmask, F32), jnp.asarray(bcast(qdec), F32),
            jnp.asarray(bcast(kdec), F32), tile_decay)


def _mixer(x, norm1_g, w_in, ret_gn_g, gmlp_ln_g, gmlp_ln_b, w_s, b_s, w_out):
    B, S, D = x.shape
    T = SEQ_TILE
    pos = jnp.arange(S, dtype=F32)
    inv = ROPE_THETA ** (-jnp.arange(0, HEAD_DIM, 2, dtype=F32) / HEAD_DIM)
    ang = pos[:, None] * inv[None, :]
    cos = jnp.concatenate([jnp.cos(ang), jnp.cos(ang)], axis=-1)
    sin = jnp.concatenate([-jnp.sin(ang), jnp.sin(ang)], axis=-1)
    dmask, qdec, kdec, tile_decay = _decay_tables(T)
    bias = jnp.repeat(b_s.T, GROUP_DIM, axis=1)

    row = lambda a: a.reshape(1, -1)
    in_specs = [
        pl.BlockSpec((1, T, D), lambda b, t: (b, t, 0)),
        _const_spec((1, D)),
        _const_spec(w_in.shape),
        pl.BlockSpec((T, LANES), lambda b, t: (t, 0)),
        pl.BlockSpec((T, LANES), lambda b, t: (t, 0)),
        _const_spec(dmask.shape),
        _const_spec(qdec.shape),
        _const_spec(kdec.shape),
        _const_spec((1, RET_WIDTH)),
        _const_spec((1, GMLP_WIDTH)),
        _const_spec((1, GMLP_WIDTH)),
        _const_spec(w_s.shape),
        _const_spec(bias.shape),
        _const_spec(w_out.shape),
    ]
    return pl.pallas_call(
        functools.partial(_mixer_kernel, tile_decay=tile_decay),
        grid=(B, S // T),
        in_specs=in_specs,
        out_specs=pl.BlockSpec((1, T, D), lambda b, t: (b, t, 0)),
        out_shape=jax.ShapeDtypeStruct((B, S, D), F32),
        scratch_shapes=[pltpu.VMEM((HEADS, HEAD_DIM, HEAD_DIM), F32)],
        compiler_params=pltpu.CompilerParams(
            dimension_semantics=("arbitrary", "arbitrary"),
            vmem_limit_bytes=V7X_VMEM_LIMIT_BYTES),
        name="token_mixer",
    )(x, row(norm1_g), w_in.astype(BF16), cos, sin, dmask, qdec, kdec,
      row(ret_gn_g), row(gmlp_ln_g), row(gmlp_ln_b), w_s, bias, w_out.astype(BF16))


def _ffn(x, norm2_g, w_gate, w_up, w_down, final_g):
    N, D = x.shape
    T = FFN_TILE
    row = lambda a: a.reshape(1, -1)
    return pl.pallas_call(
        _ffn_kernel,
        grid=(N // T,),
        in_specs=[
            pl.BlockSpec((T, D), lambda i: (i, 0)),
            _const_spec((1, D)),
            _const_spec(w_gate.shape),
            _const_spec(w_up.shape),
            _const_spec(w_down.shape),
            _const_spec((1, D)),
        ],
        out_specs=pl.BlockSpec((T, D), lambda i: (i, 0)),
        out_shape=jax.ShapeDtypeStruct((N, D), F32),
        compiler_params=pltpu.CompilerParams(
            dimension_semantics=("arbitrary",),
            vmem_limit_bytes=V7X_VMEM_LIMIT_BYTES),
        name="swiglu_ffn",
    )(x, row(norm2_g), w_gate.astype(BF16), w_up.astype(BF16), w_down.astype(BF16), row(final_g))


def kernel(x, norm1_g, w_in, ret_gn_g, gmlp_ln_g, gmlp_ln_b, w_s, b_s, w_out, norm2_g,
           w_ffn_gate, w_ffn_up, w_ffn_down, final_g):
    B, S, D = x.shape
    assert w_in.shape[0] == 1, "the FFN kernel fuses the final norm, so exactly one layer is supported"
    assert D == D_MODEL and S % SEQ_TILE == 0 and (B * S) % FFN_TILE == 0
    x1 = _mixer(x, norm1_g[0], w_in[0], ret_gn_g[0], gmlp_ln_g[0], gmlp_ln_b[0],
                w_s[0], b_s[0], w_out[0])
    out = _ffn(x1.reshape(B * S, D), norm2_g[0], w_ffn_gate[0], w_ffn_up[0], w_ffn_down[0], final_g)
    return out.reshape(B, S, D)
```

```python
import functools

import numpy as np
import jax
import jax.numpy as jnp
from jax import lax
from jax.experimental import pallas as pl
from jax.experimental.pallas import tpu as pltpu

D_MODEL = 1024
CHUNK = 64
RET_WIDTH = D_MODEL // 2
HEAD_DIM = 128
HEADS = RET_WIDTH // HEAD_DIM
GMLP_WIDTH = D_MODEL - RET_WIDTH
GROUP_DIM = 128
GROUPS = GMLP_WIDTH // GROUP_DIM
GMLP_BLOCK = 128
D_FF = 2816
ROPE_THETA = 10000.0
EPS = 1e-6

LANES = 128
V7X_VMEM_LIMIT_BYTES = 56 * 1024 * 1024

SEQ_TILE = 1024
MIX_SUB = 256
FFN_TILE = 1024
FFN_SUB = 256
FF_CHUNK = 256

F32 = jnp.float32
BF16 = jnp.bfloat16


def _rms_scale(x):
    return x * lax.rsqrt(jnp.mean(x * x, axis=-1, keepdims=True) + EPS)


def _dot(a, b):
    return jnp.dot(a, b, preferred_element_type=F32)


def _dot_nt(a, b):
    return lax.dot_general(a, b, (((1,), (1,)), ((), ())), preferred_element_type=F32)


def _dot_tn(a, b):
    return lax.dot_general(a, b, (((0,), (0,)), ((), ())), preferred_element_type=F32)


def _mixer_kernel(x_ref, g1_ref, w_in_ref, cos_ref, sin_ref, dmask_ref, qdec_ref, kdec_ref,
                  gn_g_ref, ln_g_ref, ln_b_ref, ws_ref, bs_ref, w_out_ref,
                  o_ref, state_ref, *, tile_decay):
    @pl.when(pl.program_id(1) == 0)
    def _():
        state_ref[...] = jnp.zeros_like(state_ref)

    n_sub = x_ref.shape[1] // MIX_SUB

    def rows(s):
        return pl.ds(s * MIX_SUB, MIX_SUB)

    def norm(s):
        return (_rms_scale(x_ref[0, rows(s), :]) * g1_ref[...]).astype(BF16)

    def proj(h, i):
        return _dot(h, w_in_ref[:, i * RET_WIDTH:(i + 1) * RET_WIDTH])

    def retention_head(s, hd, q, k, v):
        sl = slice(hd * HEAD_DIM, (hd + 1) * HEAD_DIM)
        cos = cos_ref[rows(s), :]
        sin = sin_ref[rows(s), :]
        qh, kh, vh = q[:, sl], k[:, sl], v[:, sl]
        qh = (qh * cos + pltpu.roll(qh, HEAD_DIM // 2, 1) * sin).astype(BF16)
        kh = (kh * cos + pltpu.roll(kh, HEAD_DIM // 2, 1) * sin).astype(BF16)
        scores = (_dot_nt(qh, kh) * dmask_ref[hd]).astype(BF16)
        inner = _dot(scores, vh.astype(BF16))
        state = state_ref[hd]
        cross = _dot(qh, state.astype(BF16)) * qdec_ref[hd]
        kv = _dot_tn(kh, (vh * kdec_ref[hd]).astype(BF16))
        state_ref[hd] = tile_decay[hd] * state + kv
        y = inner + cross
        mu = jnp.mean(y, axis=-1, keepdims=True)
        yc = y - mu
        var = jnp.mean(yc * yc, axis=-1, keepdims=True)
        return yc * lax.rsqrt(var + EPS)

    row = lax.broadcasted_iota(jnp.int32, (GMLP_BLOCK, GMLP_BLOCK), 0) // CHUNK
    col = lax.broadcasted_iota(jnp.int32, (GMLP_BLOCK, GMLP_BLOCK), 1) // CHUNK
    causal = row >= col

    h = norm(0)
    p = [proj(h, i) for i in range(6)]
    for s in range(n_sub):
        q, k, v, g, u, vg = p
        more = s + 1 < n_sub
        p_next = []
        if more:
            h = norm(s + 1)

        ret_heads = []
        for hd in range(HEADS):
            ret_heads.append(retention_head(s, hd, q, k, v))
            if more:
                p_next.append(proj(h, hd))
        ret = jnp.concatenate(ret_heads, axis=-1) * gn_g_ref[...] * (g * jax.nn.sigmoid(g))

        ug = jax.nn.gelu(u)
        vgg = jax.nn.gelu(vg)
        mu = jnp.mean(vgg, axis=-1, keepdims=True)
        vc = vgg - mu
        var = jnp.mean(vc * vc, axis=-1, keepdims=True)
        vn = (vc * lax.rsqrt(var + EPS) * ln_g_ref[...] + ln_b_ref[...]).astype(BF16)
        if more:
            p_next.append(proj(h, 4))
        mixed_blocks = []
        for m in range(MIX_SUB // GMLP_BLOCK):
            blk = slice(m * GMLP_BLOCK, (m + 1) * GMLP_BLOCK)
            groups = []
            for gi in range(GROUPS):
                w = jnp.where(causal, ws_ref[gi], 0.0).astype(BF16)
                groups.append(_dot(w, vn[blk, gi * GROUP_DIM:(gi + 1) * GROUP_DIM]))
            mixed_blocks.append(jnp.concatenate(groups, axis=-1) + bs_ref[...])
        gm = ug * jnp.concatenate(mixed_blocks, axis=0)
        if more:
            p_next.append(proj(h, 5))

        mix = jnp.concatenate([ret, gm], axis=-1).astype(BF16)
        o_ref[0, rows(s), :] = x_ref[0, rows(s), :] + _dot(mix, w_out_ref[...])
        p = p_next


def _ffn_kernel(x_ref, g2_ref, wg_ref, wu_ref, wd_ref, gf_ref, o_ref):
    n_chunks = D_FF // FF_CHUNK
    n_sub = x_ref.shape[0] // FFN_SUB

    def rows(s):
        return pl.ds(s * FFN_SUB, FFN_SUB)

    def prologue(s):
        return (_rms_scale(x_ref[rows(s), :]) * g2_ref[...]).astype(BF16)

    def epilogue(s, acc):
        o_ref[rows(s), :] = _rms_scale(x_ref[rows(s), :] + acc) * gf_ref[...]

    def gate_up(h, c):
        sl = slice(c * FF_CHUNK, (c + 1) * FF_CHUNK)
        return _dot(h, wg_ref[:, sl]), _dot(h, wu_ref[:, sl])

    h = prologue(0)
    done = None
    for s in range(n_sub):
        h_next = None
        acc = None
        gate, up = gate_up(h, 0)
        for c in range(n_chunks):
            act = (gate * jax.nn.sigmoid(gate) * up).astype(BF16)
            if c + 1 < n_chunks:
                gate, up = gate_up(h, c + 1)
            part = _dot(act, wd_ref[c * FF_CHUNK:(c + 1) * FF_CHUNK, :])
            acc = part if acc is None else acc + part
            if c == 0 and s + 1 < n_sub:
                h_next = prologue(s + 1)
            if c == 1 and done is not None:
                epilogue(*done)
        done = (s, acc)
        h = h_next
    epilogue(*done)


def _const_spec(shape):
    zeros = (0,) * len(shape)
    return pl.BlockSpec(shape, lambda *_: zeros, pipeline_mode=pl.Buffered(1))


def _decay_tables(tile):
    log_gamma = np.log(1.0 - np.power(2.0, -5.0 - np.arange(HEADS, dtype=np.float64)))
    idx = np.arange(tile, dtype=np.float64)
    diff = idx[:, None] - idx[None, :]
    same = (np.arange(tile)[:, None] // CHUNK) == (np.arange(tile)[None, :] // CHUNK)
    earlier = (np.arange(tile)[:, None] // CHUNK) > (np.arange(tile)[None, :] // CHUNK)
    expo = np.where(same, np.abs(diff), diff)
    scale = HEAD_DIM ** -0.5
    dmask = np.where(same | earlier, np.exp(log_gamma[:, None, None] * expo[None]), 0.0) * scale
    qdec = np.exp(log_gamma[:, None] * (idx + 1.0)[None, :]) * scale
    kdec = np.exp(log_gamma[:, None] * (tile - 1.0 - idx)[None, :])
    tile_decay = tuple(float(v) for v in np.exp(log_gamma * tile))
    bcast = lambda a: np.broadcast_to(a[:, :, None], (HEADS, tile, LANES))
    return (jnp.asarray(dmask, F32), jnp.asarray(bcast(qdec), F32),
            jnp.asarray(bcast(kdec), F32), tile_decay)


def _mixer(x, norm1_g, w_in, ret_gn_g, gmlp_ln_g, gmlp_ln_b, w_s, b_s, w_out):
    B, S, D = x.shape
    T = SEQ_TILE
    pos = jnp.arange(S, dtype=F32)
    inv = ROPE_THETA ** (-jnp.arange(0, HEAD_DIM, 2, dtype=F32) / HEAD_DIM)
    ang = pos[:, None] * inv[None, :]
    cos = jnp.concatenate([jnp.cos(ang), jnp.cos(ang)], axis=-1)
    sin = jnp.concatenate([-jnp.sin(ang), jnp.sin(ang)], axis=-1)
    dmask, qdec, kdec, tile_decay = _decay_tables(MIX_SUB)
    bias = jnp.repeat(b_s.T, GROUP_DIM, axis=1)

    row = lambda a: a.reshape(1, -1)
    in_specs = [
        pl.BlockSpec((1, T, D), lambda b, t: (b, t, 0)),
        _const_spec((1, D)),
        _const_spec(w_in.shape),
        pl.BlockSpec((T, LANES), lambda b, t: (t, 0)),
        pl.BlockSpec((T, LANES), lambda b, t: (t, 0)),
        _const_spec(dmask.shape),
        _const_spec(qdec.shape),
        _const_spec(kdec.shape),
        _const_spec((1, RET_WIDTH)),
        _const_spec((1, GMLP_WIDTH)),
        _const_spec((1, GMLP_WIDTH)),
        _const_spec(w_s.shape),
        _const_spec(bias.shape),
        _const_spec(w_out.shape),
    ]
    return pl.pallas_call(
        functools.partial(_mixer_kernel, tile_decay=tile_decay),
        grid=(B, S // T),
        in_specs=in_specs,
        out_specs=pl.BlockSpec((1, T, D), lambda b, t: (b, t, 0)),
        out_shape=jax.ShapeDtypeStruct((B, S, D), F32),
        scratch_shapes=[pltpu.VMEM((HEADS, HEAD_DIM, HEAD_DIM), F32)],
        compiler_params=pltpu.CompilerParams(
            dimension_semantics=("arbitrary", "arbitrary"),
            vmem_limit_bytes=V7X_VMEM_LIMIT_BYTES),
        name="token_mixer",
    )(x, row(norm1_g), w_in.astype(BF16), cos, sin, dmask, qdec, kdec,
      row(ret_gn_g), row(gmlp_ln_g), row(gmlp_ln_b), w_s, bias, w_out.astype(BF16))


def _ffn(x, norm2_g, w_gate, w_up, w_down, final_g):
    N, D = x.shape
    T = FFN_TILE
    row = lambda a: a.reshape(1, -1)
    return pl.pallas_call(
        _ffn_kernel,
        grid=(N // T,),
        in_specs=[
            pl.BlockSpec((T, D), lambda i: (i, 0)),
            _const_spec((1, D)),
            _const_spec(w_gate.shape),
            _const_spec(w_up.shape),
            _const_spec(w_down.shape),
            _const_spec((1, D)),
        ],
        out_specs=pl.BlockSpec((T, D), lambda i: (i, 0)),
        out_shape=jax.ShapeDtypeStruct((N, D), F32),
        compiler_params=pltpu.CompilerParams(
            dimension_semantics=("arbitrary",),
            vmem_limit_bytes=V7X_VMEM_LIMIT_BYTES),
        name="swiglu_ffn",
    )(x, row(norm2_g), w_gate.astype(BF16), w_up.astype(BF16), w_down.astype(BF16), row(final_g))


def kernel(x, norm1_g, w_in, ret_gn_g, gmlp_ln_g, gmlp_ln_b, w_s, b_s, w_out, norm2_g,
           w_ffn_gate, w_ffn_up, w_ffn_down, final_g):
    B, S, D = x.shape
    assert w_in.shape[0] == 1, "the FFN kernel fuses the final norm, so exactly one layer is supported"
    assert D == D_MODEL and S % SEQ_TILE == 0 and (B * S) % FFN_TILE == 0
    x1 = _mixer(x, norm1_g[0], w_in[0], ret_gn_g[0], gmlp_ln_g[0], gmlp_ln_b[0],
                w_s[0], b_s[0], w_out[0])
    out = _ffn(x1.reshape(B * S, D), norm2_g[0], w_ffn_gate[0], w_ffn_up[0], w_ffn_down[0], final_g)
    return out.reshape(B, S, D)
```

```python
import functools

import numpy as np
import jax
import jax.numpy as jnp
from jax import lax
from jax.experimental import pallas as pl
from jax.experimental.pallas import tpu as pltpu

D_MODEL = 1024
CHUNK = 64
RET_WIDTH = D_MODEL // 2
HEAD_DIM = 128
HEADS = RET_WIDTH // HEAD_DIM
GMLP_WIDTH = D_MODEL - RET_WIDTH
GROUP_DIM = 128
GROUPS = GMLP_WIDTH // GROUP_DIM
GMLP_BLOCK = 128
D_FF = 2816
ROPE_THETA = 10000.0
EPS = 1e-6

LANES = 128
V7X_VMEM_LIMIT_BYTES = 56 * 1024 * 1024

SEQ_TILE = 1024
MIX_SUB = 256
FFN_TILE = 1024
FFN_SUB = 256
FF_CHUNK = 256
WEIGHT_STAGE_ROWS = 128

F32 = jnp.float32
BF16 = jnp.bfloat16


def _rms_scale(x):
    return x * lax.rsqrt(jnp.mean(x * x, axis=-1, keepdims=True) + EPS)


def _dot(a, b):
    return jnp.dot(a, b, preferred_element_type=F32)


def _dot_nt(a, b):
    return lax.dot_general(a, b, (((1,), (1,)), ((), ())), preferred_element_type=F32)


def _dot_tn(a, b):
    return lax.dot_general(a, b, (((0,), (0,)), ((), ())), preferred_element_type=F32)


def _load_weight_as_bf16(w_hbm, w_vmem, stage, sem):
    rows = stage.shape[1]
    n_chunks = w_hbm.shape[0] // rows
    assert n_chunks * rows == w_hbm.shape[0] and stage.shape[2] == w_hbm.shape[1]

    def copy(i):
        return pltpu.make_async_copy(w_hbm.at[pl.ds(i * rows, rows), :], stage.at[i % 2], sem.at[i % 2])

    copy(0).start()
    for i in range(n_chunks):
        if i + 1 < n_chunks:
            copy(i + 1).start()
        copy(i).wait()
        w_vmem[pl.ds(i * rows, rows), :] = stage[i % 2].astype(BF16)


def _mixer_kernel(x_ref, g1_ref, w_in_hbm, cos_ref, sin_ref, dmask_ref, qdec_ref, kdec_ref,
                  gn_g_ref, ln_g_ref, ln_b_ref, ws_ref, bs_ref, w_out_hbm,
                  o_ref, state_ref, w_in_ref, w_out_ref, stage_in, stage_out, sem, *, tile_decay):
    @pl.when((pl.program_id(0) == 0) & (pl.program_id(1) == 0))
    def _():
        _load_weight_as_bf16(w_in_hbm, w_in_ref, stage_in, sem)
        _load_weight_as_bf16(w_out_hbm, w_out_ref, stage_out, sem)

    @pl.when(pl.program_id(1) == 0)
    def _():
        state_ref[...] = jnp.zeros_like(state_ref)

    n_sub = x_ref.shape[1] // MIX_SUB

    def rows(s):
        return pl.ds(s * MIX_SUB, MIX_SUB)

    def norm(s):
        return (_rms_scale(x_ref[0, rows(s), :]) * g1_ref[...]).astype(BF16)

    def proj(h, i):
        return _dot(h, w_in_ref[:, i * RET_WIDTH:(i + 1) * RET_WIDTH])

    def retention_head(s, hd, q, k, v):
        sl = slice(hd * HEAD_DIM, (hd + 1) * HEAD_DIM)
        cos = cos_ref[rows(s), :]
        sin = sin_ref[rows(s), :]
        qh, kh, vh = q[:, sl], k[:, sl], v[:, sl]
        qh = (qh * cos + pltpu.roll(qh, HEAD_DIM // 2, 1) * sin).astype(BF16)
        kh = (kh * cos + pltpu.roll(kh, HEAD_DIM // 2, 1) * sin).astype(BF16)
        scores = (_dot_nt(qh, kh) * dmask_ref[hd]).astype(BF16)
        inner = _dot(scores, vh.astype(BF16))
        state = state_ref[hd]
        cross = _dot(qh, state.astype(BF16)) * qdec_ref[hd]
        kv = _dot_tn(kh, (vh * kdec_ref[hd]).astype(BF16))
        state_ref[hd] = tile_decay[hd] * state + kv
        y = inner + cross
        mu = jnp.mean(y, axis=-1, keepdims=True)
        yc = y - mu
        var = jnp.mean(yc * yc, axis=-1, keepdims=True)
        return yc * lax.rsqrt(var + EPS)

    row = lax.broadcasted_iota(jnp.int32, (GMLP_BLOCK, GMLP_BLOCK), 0) // CHUNK
    col = lax.broadcasted_iota(jnp.int32, (GMLP_BLOCK, GMLP_BLOCK), 1) // CHUNK
    causal = row >= col

    h = norm(0)
    p = [proj(h, i) for i in range(6)]
    for s in range(n_sub):
        q, k, v, g, u, vg = p
        more = s + 1 < n_sub
        p_next = []
        if more:
            h = norm(s + 1)

        ret_heads = []
        for hd in range(HEADS):
            ret_heads.append(retention_head(s, hd, q, k, v))
            if more:
                p_next.append(proj(h, hd))
        ret = jnp.concatenate(ret_heads, axis=-1) * gn_g_ref[...] * (g * jax.nn.sigmoid(g))

        ug = jax.nn.gelu(u)
        vgg = jax.nn.gelu(vg)
        mu = jnp.mean(vgg, axis=-1, keepdims=True)
        vc = vgg - mu
        var = jnp.mean(vc * vc, axis=-1, keepdims=True)
        vn = (vc * lax.rsqrt(var + EPS) * ln_g_ref[...] + ln_b_ref[...]).astype(BF16)
        if more:
            p_next.append(proj(h, 4))
        mixed_blocks = []
        for m in range(MIX_SUB // GMLP_BLOCK):
            blk = slice(m * GMLP_BLOCK, (m + 1) * GMLP_BLOCK)
            groups = []
            for gi in range(GROUPS):
                w = jnp.where(causal, ws_ref[gi], 0.0).astype(BF16)
                groups.append(_dot(w, vn[blk, gi * GROUP_DIM:(gi + 1) * GROUP_DIM]))
            mixed_blocks.append(jnp.concatenate(groups, axis=-1) + bs_ref[...])
        gm = ug * jnp.concatenate(mixed_blocks, axis=0)
        if more:
            p_next.append(proj(h, 5))

        mix = jnp.concatenate([ret, gm], axis=-1).astype(BF16)
        o_ref[0, rows(s), :] = x_ref[0, rows(s), :] + _dot(mix, w_out_ref[...])
        p = p_next


def _ffn_kernel(x_ref, g2_ref, wg_hbm, wu_hbm, wd_hbm, gf_ref, o_ref,
                wg_ref, wu_ref, wd_ref, stage_up, stage_down, sem):
    @pl.when(pl.program_id(0) == 0)
    def _():
        _load_weight_as_bf16(wg_hbm, wg_ref, stage_up, sem)
        _load_weight_as_bf16(wu_hbm, wu_ref, stage_up, sem)
        _load_weight_as_bf16(wd_hbm, wd_ref, stage_down, sem)

    n_chunks = D_FF // FF_CHUNK
    n_sub = x_ref.shape[0] // FFN_SUB

    def rows(s):
        return pl.ds(s * FFN_SUB, FFN_SUB)

    def prologue(s):
        return (_rms_scale(x_ref[rows(s), :]) * g2_ref[...]).astype(BF16)

    def epilogue(s, acc):
        o_ref[rows(s), :] = _rms_scale(x_ref[rows(s), :] + acc) * gf_ref[...]

    def gate_up(h, c):
        sl = slice(c * FF_CHUNK, (c + 1) * FF_CHUNK)
        return _dot(h, wg_ref[:, sl]), _dot(h, wu_ref[:, sl])

    h = prologue(0)
    done = None
    for s in range(n_sub):
        h_next = None
        acc = None
        gate, up = gate_up(h, 0)
        for c in range(n_chunks):
            act = (gate * jax.nn.sigmoid(gate) * up).astype(BF16)
            if c + 1 < n_chunks:
                gate, up = gate_up(h, c + 1)
            part = _dot(act, wd_ref[c * FF_CHUNK:(c + 1) * FF_CHUNK, :])
            acc = part if acc is None else acc + part
            if c == 0 and s + 1 < n_sub:
                h_next = prologue(s + 1)
            if c == 1 and done is not None:
                epilogue(*done)
        done = (s, acc)
        h = h_next
    epilogue(*done)


def _const_spec(shape):
    zeros = (0,) * len(shape)
    return pl.BlockSpec(shape, lambda *_: zeros, pipeline_mode=pl.Buffered(1))


def _decay_tables(tile):
    log_gamma = np.log(1.0 - np.power(2.0, -5.0 - np.arange(HEADS, dtype=np.float64)))
    idx = np.arange(tile, dtype=np.float64)
    diff = idx[:, None] - idx[None, :]
    same = (np.arange(tile)[:, None] // CHUNK) == (np.arange(tile)[None, :] // CHUNK)
    earlier = (np.arange(tile)[:, None] // CHUNK) > (np.arange(tile)[None, :] // CHUNK)
    expo = np.where(same, np.abs(diff), diff)
    scale = HEAD_DIM ** -0.5
    dmask = np.where(same | earlier, np.exp(log_gamma[:, None, None] * expo[None]), 0.0) * scale
    qdec = np.exp(log_gamma[:, None] * (idx + 1.0)[None, :]) * scale
    kdec = np.exp(log_gamma[:, None] * (tile - 1.0 - idx)[None, :])
    tile_decay = tuple(float(v) for v in np.exp(log_gamma * tile))
    bcast = lambda a: np.broadcast_to(a[:, :, None], (HEADS, tile, LANES))
    return (jnp.asarray(dmask, F32), jnp.asarray(bcast(qdec), F32),
            jnp.asarray(bcast(kdec), F32), tile_decay)


def _mixer(x, norm1_g, w_in, ret_gn_g, gmlp_ln_g, gmlp_ln_b, w_s, b_s, w_out):
    B, S, D = x.shape
    T = SEQ_TILE
    pos = jnp.arange(S, dtype=F32)
    inv = ROPE_THETA ** (-jnp.arange(0, HEAD_DIM, 2, dtype=F32) / HEAD_DIM)
    ang = pos[:, None] * inv[None, :]
    cos = jnp.concatenate([jnp.cos(ang), jnp.cos(ang)], axis=-1)
    sin = jnp.concatenate([-jnp.sin(ang), jnp.sin(ang)], axis=-1)
    dmask, qdec, kdec, tile_decay = _decay_tables(MIX_SUB)
    bias = jnp.repeat(b_s.T, GROUP_DIM, axis=1)

    row = lambda a: a.reshape(1, -1)
    hbm = pl.BlockSpec(memory_space=pl.ANY)
    in_specs = [
        pl.BlockSpec((1, T, D), lambda b, t: (b, t, 0)),
        _const_spec((1, D)),
        hbm,
        pl.BlockSpec((T, LANES), lambda b, t: (t, 0)),
        pl.BlockSpec((T, LANES), lambda b, t: (t, 0)),
        _const_spec(dmask.shape),
        _const_spec(qdec.shape),
        _const_spec(kdec.shape),
        _const_spec((1, RET_WIDTH)),
        _const_spec((1, GMLP_WIDTH)),
        _const_spec((1, GMLP_WIDTH)),
        _const_spec(w_s.shape),
        _const_spec(bias.shape),
        hbm,
    ]
    return pl.pallas_call(
        functools.partial(_mixer_kernel, tile_decay=tile_decay),
        grid=(B, S // T),
        in_specs=in_specs,
        out_specs=pl.BlockSpec((1, T, D), lambda b, t: (b, t, 0)),
        out_shape=jax.ShapeDtypeStruct((B, S, D), F32),
        scratch_shapes=[
            pltpu.VMEM((HEADS, HEAD_DIM, HEAD_DIM), F32),
            pltpu.VMEM(w_in.shape, BF16),
            pltpu.VMEM(w_out.shape, BF16),
            pltpu.VMEM((2, WEIGHT_STAGE_ROWS, w_in.shape[1]), F32),
            pltpu.VMEM((2, WEIGHT_STAGE_ROWS, w_out.shape[1]), F32),
            pltpu.SemaphoreType.DMA((2,)),
        ],
        compiler_params=pltpu.CompilerParams(
            dimension_semantics=("arbitrary", "arbitrary"),
            vmem_limit_bytes=V7X_VMEM_LIMIT_BYTES),
        name="token_mixer",
    )(x, row(norm1_g), w_in, cos, sin, dmask, qdec, kdec,
      row(ret_gn_g), row(gmlp_ln_g), row(gmlp_ln_b), w_s, bias, w_out)


def _ffn(x, norm2_g, w_gate, w_up, w_down, final_g):
    N, D = x.shape
    T = FFN_TILE
    row = lambda a: a.reshape(1, -1)
    hbm = pl.BlockSpec(memory_space=pl.ANY)
    return pl.pallas_call(
        _ffn_kernel,
        grid=(N // T,),
        in_specs=[
            pl.BlockSpec((T, D), lambda i: (i, 0)),
            _const_spec((1, D)),
            hbm,
            hbm,
            hbm,
            _const_spec((1, D)),
        ],
        out_specs=pl.BlockSpec((T, D), lambda i: (i, 0)),
        out_shape=jax.ShapeDtypeStruct((N, D), F32),
        scratch_shapes=[
            pltpu.VMEM(w_gate.shape, BF16),
            pltpu.VMEM(w_up.shape, BF16),
            pltpu.VMEM(w_down.shape, BF16),
            pltpu.VMEM((2, WEIGHT_STAGE_ROWS, w_gate.shape[1]), F32),
            pltpu.VMEM((2, WEIGHT_STAGE_ROWS, w_down.shape[1]), F32),
            pltpu.SemaphoreType.DMA((2,)),
        ],
        compiler_params=pltpu.CompilerParams(
            dimension_semantics=("arbitrary",),
            vmem_limit_bytes=V7X_VMEM_LIMIT_BYTES),
        name="swiglu_ffn",
    )(x, row(norm2_g), w_gate, w_up, w_down, row(final_g))


def kernel(x, norm1_g, w_in, ret_gn_g, gmlp_ln_g, gmlp_ln_b, w_s, b_s, w_out, norm2_g,
           w_ffn_gate, w_ffn_up, w_ffn_down, final_g):
    B, S, D = x.shape
    assert w_in.shape[0] == 1, "the FFN kernel fuses the final norm, so exactly one layer is supported"
    assert D == D_MODEL and S % SEQ_TILE == 0 and (B * S) % FFN_TILE == 0
    x1 = _mixer(x, norm1_g[0], w_in[0], ret_gn_g[0], gmlp_ln_g[0], gmlp_ln_b[0],
                w_s[0], b_s[0], w_out[0])
    out = _ffn(x1.reshape(B * S, D), norm2_g[0], w_ffn_gate[0], w_ffn_up[0], w_ffn_down[0], final_g)
    return out.reshape(B, S, D)
```

```python
import functools

import numpy as np
import jax
import jax.numpy as jnp
from jax import lax
from jax.experimental import pallas as pl
from jax.experimental.pallas import tpu as pltpu

D_MODEL = 1024
CHUNK = 64
RET_WIDTH = D_MODEL // 2
HEAD_DIM = 128
HEADS = RET_WIDTH // HEAD_DIM
GMLP_WIDTH = D_MODEL - RET_WIDTH
GROUP_DIM = 128
GROUPS = GMLP_WIDTH // GROUP_DIM
GMLP_BLOCK = 128
D_FF = 2816
ROPE_THETA = 10000.0
EPS = 1e-6

LANES = 128
V7X_VMEM_LIMIT_BYTES = 56 * 1024 * 1024

SEQ_TILE = 1024
MIX_SUB = 256
FFN_TILE = 1024
FFN_SUB = 256
FF_CHUNK = 256
WEIGHT_STAGE_ROWS = 128
WEIGHT_STAGE_SLOTS = 4
FFN_DOWN_ROWS = 128
BF16_SUBLANES = 16

F32 = jnp.float32
BF16 = jnp.bfloat16


def _rms_scale(x):
    return x * lax.rsqrt(jnp.mean(x * x, axis=-1, keepdims=True) + EPS)


def _dot(a, b):
    return jnp.dot(a, b, preferred_element_type=F32)


def _dot_nt(a, b):
    return lax.dot_general(a, b, (((1,), (1,)), ((), ())), preferred_element_type=F32)


def _dot_tn(a, b):
    return lax.dot_general(a, b, (((0,), (0,)), ((), ())), preferred_element_type=F32)


def _load_weight_as_bf16(w_hbm, w_vmem, stage, sem):
    slots, rows = stage.shape[0], stage.shape[1]
    n_chunks = w_hbm.shape[0] // rows
    assert n_chunks * rows == w_hbm.shape[0] and stage.shape[2] == w_hbm.shape[1]

    def copy(i):
        return pltpu.make_async_copy(w_hbm.at[pl.ds(i * rows, rows), :], stage.at[i % slots], sem.at[i % slots])

    for i in range(min(slots - 1, n_chunks)):
        copy(i).start()
    for i in range(n_chunks):
        if i + slots - 1 < n_chunks:
            copy(i + slots - 1).start()
        copy(i).wait()
        w_vmem[pl.ds(i * rows, rows), :] = stage[i % slots].astype(BF16)


def _mixer_kernel(x_ref, g1_ref, w_in_hbm, cos_ref, sin_ref, dmask_ref, qdec_ref, kdec_ref,
                  gn_g_ref, ln_g_ref, ln_b_ref, ws_ref, bs_ref, w_out_hbm,
                  ffn_gate_ref, ffn_up_ref, ffn_down_ref,
                  o_ref, ffn_gate_bf_ref, ffn_up_bf_ref, ffn_down_bf_ref,
                  state_ref, w_in_ref, w_out_ref, stage_in, stage_out, sem, *, tile_decay):
    @pl.when((pl.program_id(0) == 0) & (pl.program_id(1) == 0))
    def _():
        _load_weight_as_bf16(w_in_hbm, w_in_ref, stage_in, sem)
        _load_weight_as_bf16(w_out_hbm, w_out_ref, stage_out, sem)

    ffn_gate_bf_ref[...] = ffn_gate_ref[...].astype(BF16)
    ffn_up_bf_ref[...] = ffn_up_ref[...].astype(BF16)
    ffn_down_bf_ref[...] = ffn_down_ref[...].astype(BF16)

    @pl.when(pl.program_id(1) == 0)
    def _():
        state_ref[...] = jnp.zeros_like(state_ref)

    n_sub = x_ref.shape[1] // MIX_SUB

    def rows(s):
        return pl.ds(s * MIX_SUB, MIX_SUB)

    def norm(s):
        return (_rms_scale(x_ref[0, rows(s), :]) * g1_ref[...]).astype(BF16)

    def proj(h, i):
        return _dot(h, w_in_ref[:, i * RET_WIDTH:(i + 1) * RET_WIDTH])

    def retention_head(s, hd, q, k, v):
        sl = slice(hd * HEAD_DIM, (hd + 1) * HEAD_DIM)
        cos = cos_ref[rows(s), :]
        sin = sin_ref[rows(s), :]
        qh, kh, vh = q[:, sl], k[:, sl], v[:, sl]
        qh = (qh * cos + pltpu.roll(qh, HEAD_DIM // 2, 1) * sin).astype(BF16)
        kh = (kh * cos + pltpu.roll(kh, HEAD_DIM // 2, 1) * sin).astype(BF16)
        scores = (_dot_nt(qh, kh) * dmask_ref[hd]).astype(BF16)
        inner = _dot(scores, vh.astype(BF16))
        state = state_ref[hd]
        cross = _dot(qh, state.astype(BF16)) * qdec_ref[hd]
        kv = _dot_tn(kh, (vh * kdec_ref[hd]).astype(BF16))
        state_ref[hd] = tile_decay[hd] * state + kv
        y = inner + cross
        mu = jnp.mean(y, axis=-1, keepdims=True)
        yc = y - mu
        var = jnp.mean(yc * yc, axis=-1, keepdims=True)
        return yc * lax.rsqrt(var + EPS)

    row = lax.broadcasted_iota(jnp.int32, (GMLP_BLOCK, GMLP_BLOCK), 0) // CHUNK
    col = lax.broadcasted_iota(jnp.int32, (GMLP_BLOCK, GMLP_BLOCK), 1) // CHUNK
    causal = row >= col

    h = norm(0)
    p = [proj(h, i) for i in range(6)]
    for s in range(n_sub):
        q, k, v, g, u, vg = p
        more = s + 1 < n_sub
        p_next = []
        if more:
            h = norm(s + 1)

        ret_heads = []
        for hd in range(HEADS):
            ret_heads.append(retention_head(s, hd, q, k, v))
            if more:
                p_next.append(proj(h, hd))
        ret = jnp.concatenate(ret_heads, axis=-1) * gn_g_ref[...] * (g * jax.nn.sigmoid(g))

        ug = jax.nn.gelu(u)
        vgg = jax.nn.gelu(vg)
        mu = jnp.mean(vgg, axis=-1, keepdims=True)
        vc = vgg - mu
        var = jnp.mean(vc * vc, axis=-1, keepdims=True)
        vn = (vc * lax.rsqrt(var + EPS) * ln_g_ref[...] + ln_b_ref[...]).astype(BF16)
        if more:
            p_next.append(proj(h, 4))
        mixed_blocks = []
        for m in range(MIX_SUB // GMLP_BLOCK):
            blk = slice(m * GMLP_BLOCK, (m + 1) * GMLP_BLOCK)
            groups = []
            for gi in range(GROUPS):
                w = jnp.where(causal, ws_ref[gi], 0.0).astype(BF16)
                groups.append(_dot(w, vn[blk, gi * GROUP_DIM:(gi + 1) * GROUP_DIM]))
            mixed_blocks.append(jnp.concatenate(groups, axis=-1) + bs_ref[...])
        gm = ug * jnp.concatenate(mixed_blocks, axis=0)
        if more:
            p_next.append(proj(h, 5))

        mix = jnp.concatenate([ret, gm], axis=-1).astype(BF16)
        o_ref[0, rows(s), :] = x_ref[0, rows(s), :] + _dot(mix, w_out_ref[...])
        p = p_next


def _ffn_kernel(x_ref, g2_ref, wg_ref, wu_ref, wd_ref, gf_ref, o_ref):
    n_chunks = D_FF // FF_CHUNK
    n_sub = x_ref.shape[0] // FFN_SUB

    def rows(s):
        return pl.ds(s * FFN_SUB, FFN_SUB)

    def prologue(s):
        return (_rms_scale(x_ref[rows(s), :]) * g2_ref[...]).astype(BF16)

    def epilogue(s, acc):
        o_ref[rows(s), :] = _rms_scale(x_ref[rows(s), :] + acc) * gf_ref[...]

    def gate_up(h, c):
        sl = slice(c * FF_CHUNK, (c + 1) * FF_CHUNK)
        return _dot(h, wg_ref[:, sl]), _dot(h, wu_ref[:, sl])

    h = prologue(0)
    done = None
    for s in range(n_sub):
        h_next = None
        acc = None
        gate, up = gate_up(h, 0)
        for c in range(n_chunks):
            act = (gate * jax.nn.sigmoid(gate) * up).astype(BF16)
            if c + 1 < n_chunks:
                gate, up = gate_up(h, c + 1)
            part = _dot(act, wd_ref[c * FF_CHUNK:(c + 1) * FF_CHUNK, :])
            acc = part if acc is None else acc + part
            if c == 0 and s + 1 < n_sub:
                h_next = prologue(s + 1)
            if c == 1 and done is not None:
                epilogue(*done)
        done = (s, acc)
        h = h_next
    epilogue(*done)


def _const_spec(shape):
    zeros = (0,) * len(shape)
    return pl.BlockSpec(shape, lambda *_: zeros, pipeline_mode=pl.Buffered(1))


def _decay_tables(tile):
    log_gamma = np.log(1.0 - np.power(2.0, -5.0 - np.arange(HEADS, dtype=np.float64)))
    idx = np.arange(tile, dtype=np.float64)
    diff = idx[:, None] - idx[None, :]
    same = (np.arange(tile)[:, None] // CHUNK) == (np.arange(tile)[None, :] // CHUNK)
    earlier = (np.arange(tile)[:, None] // CHUNK) > (np.arange(tile)[None, :] // CHUNK)
    expo = np.where(same, np.abs(diff), diff)
    scale = HEAD_DIM ** -0.5
    dmask = np.where(same | earlier, np.exp(log_gamma[:, None, None] * expo[None]), 0.0) * scale
    qdec = np.exp(log_gamma[:, None] * (idx + 1.0)[None, :]) * scale
    kdec = np.exp(log_gamma[:, None] * (tile - 1.0 - idx)[None, :])
    tile_decay = tuple(float(v) for v in np.exp(log_gamma * tile))
    bcast = lambda a: np.broadcast_to(a[:, :, None], (HEADS, tile, LANES))
    return (jnp.asarray(dmask, F32), jnp.asarray(bcast(qdec), F32),
            jnp.asarray(bcast(kdec), F32), tile_decay)


def _rope_tables(seq):
    inv = ROPE_THETA ** (-np.arange(0, HEAD_DIM, 2, dtype=np.float64) / HEAD_DIM)
    ang = np.arange(seq, dtype=np.float64)[:, None] * inv[None, :]
    cos = np.concatenate([np.cos(ang), np.cos(ang)], axis=-1)
    sin = np.concatenate([-np.sin(ang), np.sin(ang)], axis=-1)
    return jnp.asarray(cos, F32), jnp.asarray(sin, F32)


def _mixer(x, norm1_g, w_in, ret_gn_g, gmlp_ln_g, gmlp_ln_b, w_s, b_s, w_out, w_gate, w_up, w_down):
    B, S, D = x.shape
    T = SEQ_TILE
    steps = B * (S // T)
    cos, sin = _rope_tables(S)
    dmask, qdec, kdec, tile_decay = _decay_tables(MIX_SUB)
    bias = jnp.repeat(b_s.T, GROUP_DIM, axis=1)

    up_rows = D // steps
    down_slices = D_FF // FFN_DOWN_ROWS
    assert up_rows * steps == D and up_rows % BF16_SUBLANES == 0
    assert down_slices * FFN_DOWN_ROWS == D_FF and down_slices <= steps
    step_of = lambda b, t: b * (S // T) + t
    up_spec = pl.BlockSpec((up_rows, D_FF), lambda b, t: (step_of(b, t), 0))
    down_spec = pl.BlockSpec((FFN_DOWN_ROWS, D),
                             lambda b, t: (jnp.minimum(step_of(b, t), down_slices - 1), 0))

    row = lambda a: a.reshape(1, -1)
    hbm = pl.BlockSpec(memory_space=pl.ANY)
    in_specs = [
        pl.BlockSpec((1, T, D), lambda b, t: (b, t, 0)),
        _const_spec((1, D)),
        hbm,
        pl.BlockSpec((T, LANES), lambda b, t: (t, 0)),
        pl.BlockSpec((T, LANES), lambda b, t: (t, 0)),
        _const_spec(dmask.shape),
        _const_spec(qdec.shape),
        _const_spec(kdec.shape),
        _const_spec((1, RET_WIDTH)),
        _const_spec((1, GMLP_WIDTH)),
        _const_spec((1, GMLP_WIDTH)),
        _const_spec(w_s.shape),
        _const_spec(bias.shape),
        hbm,
        up_spec,
        up_spec,
        down_spec,
    ]
    return pl.pallas_call(
        functools.partial(_mixer_kernel, tile_decay=tile_decay),
        grid=(B, S // T),
        in_specs=in_specs,
        out_specs=[pl.BlockSpec((1, T, D), lambda b, t: (b, t, 0)), up_spec, up_spec, down_spec],
        out_shape=[jax.ShapeDtypeStruct((B, S, D), F32),
                   jax.ShapeDtypeStruct(w_gate.shape, BF16),
                   jax.ShapeDtypeStruct(w_up.shape, BF16),
                   jax.ShapeDtypeStruct(w_down.shape, BF16)],
        scratch_shapes=[
            pltpu.VMEM((HEADS, HEAD_DIM, HEAD_DIM), F32),
            pltpu.VMEM(w_in.shape, BF16),
            pltpu.VMEM(w_out.shape, BF16),
            pltpu.VMEM((WEIGHT_STAGE_SLOTS, WEIGHT_STAGE_ROWS, w_in.shape[1]), F32),
            pltpu.VMEM((WEIGHT_STAGE_SLOTS, WEIGHT_STAGE_ROWS, w_out.shape[1]), F32),
            pltpu.SemaphoreType.DMA((WEIGHT_STAGE_SLOTS,)),
        ],
        compiler_params=pltpu.CompilerParams(
            dimension_semantics=("arbitrary", "arbitrary"),
            vmem_limit_bytes=V7X_VMEM_LIMIT_BYTES),
        name="token_mixer",
    )(x, row(norm1_g), w_in, cos, sin, dmask, qdec, kdec,
      row(ret_gn_g), row(gmlp_ln_g), row(gmlp_ln_b), w_s, bias, w_out, w_gate, w_up, w_down)


def _ffn(x, norm2_g, w_gate, w_up, w_down, final_g):
    N, D = x.shape
    T = FFN_TILE
    row = lambda a: a.reshape(1, -1)
    return pl.pallas_call(
        _ffn_kernel,
        grid=(N // T,),
        in_specs=[
            pl.BlockSpec((T, D), lambda i: (i, 0)),
            _const_spec((1, D)),
            _const_spec(w_gate.shape),
            _const_spec(w_up.shape),
            _const_spec(w_down.shape),
            _const_spec((1, D)),
        ],
        out_specs=pl.BlockSpec((T, D), lambda i: (i, 0)),
        out_shape=jax.ShapeDtypeStruct((N, D), F32),
        compiler_params=pltpu.CompilerParams(
            dimension_semantics=("arbitrary",),
            vmem_limit_bytes=V7X_VMEM_LIMIT_BYTES),
        name="swiglu_ffn",
    )(x, row(norm2_g), w_gate, w_up, w_down, row(final_g))


def kernel(x, norm1_g, w_in, ret_gn_g, gmlp_ln_g, gmlp_ln_b, w_s, b_s, w_out, norm2_g,
           w_ffn_gate, w_ffn_up, w_ffn_down, final_g):
    B, S, D = x.shape
    assert w_in.shape[0] == 1, "the FFN kernel fuses the final norm, so exactly one layer is supported"
    assert D == D_MODEL and S % SEQ_TILE == 0 and (B * S) % FFN_TILE == 0
    x1, w_gate, w_up, w_down = _mixer(
        x, norm1_g[0], w_in[0], ret_gn_g[0], gmlp_ln_g[0], gmlp_ln_b[0], w_s[0], b_s[0], w_out[0],
        w_ffn_gate[0], w_ffn_up[0], w_ffn_down[0])
    out = _ffn(x1.reshape(B * S, D), norm2_g[0], w_gate, w_up, w_down, final_g)
    return out.reshape(B, S, D)
```

```python
import functools

import numpy as np
import jax
import jax.numpy as jnp
from jax import lax
from jax.experimental import pallas as pl
from jax.experimental.pallas import tpu as pltpu

D_MODEL = 1024
CHUNK = 64
RET_WIDTH = D_MODEL // 2
HEAD_DIM = 128
HEADS = RET_WIDTH // HEAD_DIM
GMLP_WIDTH = D_MODEL - RET_WIDTH
GROUP_DIM = 128
GROUPS = GMLP_WIDTH // GROUP_DIM
GMLP_BLOCK = 128
D_FF = 2816
ROPE_THETA = 10000.0
EPS = 1e-6

LANES = 128
V7X_VMEM_LIMIT_BYTES = 56 * 1024 * 1024

SEQ_TILE = 1024
MIX_SUB = 256
FFN_TILE = 1024
FFN_SUB = 256
FF_CHUNK = 256
WEIGHT_STAGE_ROWS = 128
WEIGHT_STAGE_SLOTS = 4
FFN_DOWN_ROWS = 128
BF16_SUBLANES = 16

F32 = jnp.float32
BF16 = jnp.bfloat16


def _rms_scale(x):
    return x * lax.rsqrt(jnp.mean(x * x, axis=-1, keepdims=True) + EPS)


def _dot(a, b):
    return jnp.dot(a, b, preferred_element_type=F32)


def _dot_nt(a, b):
    return lax.dot_general(a, b, (((1,), (1,)), ((), ())), preferred_element_type=F32)


def _dot_tn(a, b):
    return lax.dot_general(a, b, (((0,), (0,)), ((), ())), preferred_element_type=F32)


def _load_weight_as_bf16(w_hbm, w_vmem, stage, sem):
    slots, rows = stage.shape[0], stage.shape[1]
    n_chunks = w_hbm.shape[0] // rows
    assert n_chunks * rows == w_hbm.shape[0] and stage.shape[2] == w_hbm.shape[1]

    def copy(i):
        return pltpu.make_async_copy(w_hbm.at[pl.ds(i * rows, rows), :], stage.at[i % slots], sem.at[i % slots])

    for i in range(min(slots - 1, n_chunks)):
        copy(i).start()
    for i in range(n_chunks):
        if i + slots - 1 < n_chunks:
            copy(i + slots - 1).start()
        copy(i).wait()
        w_vmem[pl.ds(i * rows, rows), :] = stage[i % slots].astype(BF16)


def _mixer_kernel(x_ref, xn_ref, g1_ref, w_in_hbm, cos_ref, sin_ref, dmask_ref, qdec_ref, kdec_ref,
                  gn_g_ref, ln_g_ref, ln_b_ref, ws_ref, bs_ref, w_out_hbm,
                  ffn_gate_ref, ffn_up_ref, ffn_down_ref,
                  o_ref, ffn_gate_bf_ref, ffn_up_bf_ref, ffn_down_bf_ref,
                  state_ref, p_carry, w_in_ref, w_out_ref, stage_in, stage_out, sem, *, tile_decay):
    @pl.when((pl.program_id(0) == 0) & (pl.program_id(1) == 0))
    def _():
        _load_weight_as_bf16(w_in_hbm, w_in_ref, stage_in, sem)
        _load_weight_as_bf16(w_out_hbm, w_out_ref, stage_out, sem)

    ffn_gate_bf_ref[...] = ffn_gate_ref[...].astype(BF16)
    ffn_up_bf_ref[...] = ffn_up_ref[...].astype(BF16)
    ffn_down_bf_ref[...] = ffn_down_ref[...].astype(BF16)

    @pl.when(pl.program_id(1) == 0)
    def _():
        state_ref[...] = jnp.zeros_like(state_ref)

    n_sub = x_ref.shape[1] // MIX_SUB

    def rows(s):
        return pl.ds(s * MIX_SUB, MIX_SUB)

    def norm(s):
        return (_rms_scale(x_ref[0, rows(s), :]) * g1_ref[...]).astype(BF16)

    def proj(h, i):
        return _dot(h, w_in_ref[:, i * RET_WIDTH:(i + 1) * RET_WIDTH])

    def retention_head(s, hd, q, k, v):
        sl = slice(hd * HEAD_DIM, (hd + 1) * HEAD_DIM)
        cos = cos_ref[rows(s), :]
        sin = sin_ref[rows(s), :]
        qh, kh, vh = q[:, sl], k[:, sl], v[:, sl]
        qh = (qh * cos + pltpu.roll(qh, HEAD_DIM // 2, 1) * sin).astype(BF16)
        kh = (kh * cos + pltpu.roll(kh, HEAD_DIM // 2, 1) * sin).astype(BF16)
        scores = (_dot_nt(qh, kh) * dmask_ref[hd]).astype(BF16)
        inner = _dot(scores, vh.astype(BF16))
        state = state_ref[hd]
        cross = _dot(qh, state.astype(BF16)) * qdec_ref[hd]
        kv = _dot_tn(kh, (vh * kdec_ref[hd]).astype(BF16))
        state_ref[hd] = tile_decay[hd] * state + kv
        y = inner + cross
        mu = jnp.mean(y, axis=-1, keepdims=True)
        yc = y - mu
        var = jnp.mean(yc * yc, axis=-1, keepdims=True)
        return yc * lax.rsqrt(var + EPS)

    row = lax.broadcasted_iota(jnp.int32, (GMLP_BLOCK, GMLP_BLOCK), 0) // CHUNK
    col = lax.broadcasted_iota(jnp.int32, (GMLP_BLOCK, GMLP_BLOCK), 1) // CHUNK
    causal = row >= col

    def section(i):
        return slice(i * RET_WIDTH, (i + 1) * RET_WIDTH)

    @pl.when((pl.program_id(0) == 0) & (pl.program_id(1) == 0))
    def _():
        h0 = norm(0)
        for i in range(6):
            p_carry[:, section(i)] = proj(h0, i)

    p = [p_carry[:, section(i)] for i in range(6)]
    for s in range(n_sub):
        q, k, v, g, u, vg = p
        last = s + 1 == n_sub
        p_next = []
        if last:
            h = (_rms_scale(xn_ref[0]) * g1_ref[...]).astype(BF16)
        else:
            h = norm(s + 1)

        def next_proj(i):
            if last:
                p_carry[:, section(i)] = proj(h, i)
            else:
                p_next.append(proj(h, i))

        ret_heads = []
        for hd in range(HEADS):
            ret_heads.append(retention_head(s, hd, q, k, v))
            next_proj(hd)
        ret = jnp.concatenate(ret_heads, axis=-1) * gn_g_ref[...] * (g * jax.nn.sigmoid(g))

        ug = jax.nn.gelu(u)
        vgg = jax.nn.gelu(vg)
        mu = jnp.mean(vgg, axis=-1, keepdims=True)
        vc = vgg - mu
        var = jnp.mean(vc * vc, axis=-1, keepdims=True)
        vn = (vc * lax.rsqrt(var + EPS) * ln_g_ref[...] + ln_b_ref[...]).astype(BF16)
        next_proj(4)
        mixed_blocks = []
        for m in range(MIX_SUB // GMLP_BLOCK):
            blk = slice(m * GMLP_BLOCK, (m + 1) * GMLP_BLOCK)
            groups = []
            for gi in range(GROUPS):
                w = jnp.where(causal, ws_ref[gi], 0.0).astype(BF16)
                groups.append(_dot(w, vn[blk, gi * GROUP_DIM:(gi + 1) * GROUP_DIM]))
            mixed_blocks.append(jnp.concatenate(groups, axis=-1) + bs_ref[...])
        gm = ug * jnp.concatenate(mixed_blocks, axis=0)
        next_proj(5)

        mix = jnp.concatenate([ret, gm], axis=-1).astype(BF16)
        o_ref[0, rows(s), :] = x_ref[0, rows(s), :] + _dot(mix, w_out_ref[...])
        p = p_next


def _ffn_kernel(x_ref, g2_ref, wg_ref, wu_ref, wd_ref, gf_ref, o_ref):
    n_chunks = D_FF // FF_CHUNK
    n_sub = x_ref.shape[0] // FFN_SUB

    def rows(s):
        return pl.ds(s * FFN_SUB, FFN_SUB)

    def prologue(s):
        return (_rms_scale(x_ref[rows(s), :]) * g2_ref[...]).astype(BF16)

    def epilogue(s, acc):
        o_ref[rows(s), :] = _rms_scale(x_ref[rows(s), :] + acc) * gf_ref[...]

    def gate_up(h, c):
        sl = slice(c * FF_CHUNK, (c + 1) * FF_CHUNK)
        return _dot(h, wg_ref[:, sl]), _dot(h, wu_ref[:, sl])

    h = prologue(0)
    done = None
    for s in range(n_sub):
        h_next = None
        acc = None
        gate, up = gate_up(h, 0)
        for c in range(n_chunks):
            act = (gate * jax.nn.sigmoid(gate) * up).astype(BF16)
            if c + 1 < n_chunks:
                gate, up = gate_up(h, c + 1)
            part = _dot(act, wd_ref[c * FF_CHUNK:(c + 1) * FF_CHUNK, :])
            acc = part if acc is None else acc + part
            if c == 0 and s + 1 < n_sub:
                h_next = prologue(s + 1)
            if c == 1 and done is not None:
                epilogue(*done)
        done = (s, acc)
        h = h_next
    epilogue(*done)


def _const_spec(shape):
    zeros = (0,) * len(shape)
    return pl.BlockSpec(shape, lambda *_: zeros, pipeline_mode=pl.Buffered(1))


def _decay_tables(tile):
    log_gamma = np.log(1.0 - np.power(2.0, -5.0 - np.arange(HEADS, dtype=np.float64)))
    idx = np.arange(tile, dtype=np.float64)
    diff = idx[:, None] - idx[None, :]
    same = (np.arange(tile)[:, None] // CHUNK) == (np.arange(tile)[None, :] // CHUNK)
    earlier = (np.arange(tile)[:, None] // CHUNK) > (np.arange(tile)[None, :] // CHUNK)
    expo = np.where(same, np.abs(diff), diff)
    scale = HEAD_DIM ** -0.5
    dmask = np.where(same | earlier, np.exp(log_gamma[:, None, None] * expo[None]), 0.0) * scale
    qdec = np.exp(log_gamma[:, None] * (idx + 1.0)[None, :]) * scale
    kdec = np.exp(log_gamma[:, None] * (tile - 1.0 - idx)[None, :])
    tile_decay = tuple(float(v) for v in np.exp(log_gamma * tile))
    bcast = lambda a: np.broadcast_to(a[:, :, None], (HEADS, tile, LANES))
    return (jnp.asarray(dmask, F32), jnp.asarray(bcast(qdec), F32),
            jnp.asarray(bcast(kdec), F32), tile_decay)


def _rope_tables(seq):
    inv = ROPE_THETA ** (-np.arange(0, HEAD_DIM, 2, dtype=np.float64) / HEAD_DIM)
    ang = np.arange(seq, dtype=np.float64)[:, None] * inv[None, :]
    cos = np.concatenate([np.cos(ang), np.cos(ang)], axis=-1)
    sin = np.concatenate([-np.sin(ang), np.sin(ang)], axis=-1)
    return jnp.asarray(cos, F32), jnp.asarray(sin, F32)


def _mixer(x, norm1_g, w_in, ret_gn_g, gmlp_ln_g, gmlp_ln_b, w_s, b_s, w_out, w_gate, w_up, w_down):
    B, S, D = x.shape
    T = SEQ_TILE
    steps = B * (S // T)
    cos, sin = _rope_tables(S)
    dmask, qdec, kdec, tile_decay = _decay_tables(MIX_SUB)
    bias = jnp.repeat(b_s.T, GROUP_DIM, axis=1)

    up_rows = D // steps
    down_slices = D_FF // FFN_DOWN_ROWS
    assert up_rows * steps == D and up_rows % BF16_SUBLANES == 0
    assert down_slices * FFN_DOWN_ROWS == D_FF and down_slices <= steps
    step_of = lambda b, t: b * (S // T) + t
    up_spec = pl.BlockSpec((up_rows, D_FF), lambda b, t: (step_of(b, t), 0))
    down_spec = pl.BlockSpec((FFN_DOWN_ROWS, D),
                             lambda b, t: (jnp.minimum(step_of(b, t), down_slices - 1), 0))

    n_t = S // T

    def next_rows(b, t):
        nb = jnp.minimum(b + (t + 1) // n_t, B - 1)
        return nb, ((t + 1) % n_t) * (T // MIX_SUB), 0

    row = lambda a: a.reshape(1, -1)
    hbm = pl.BlockSpec(memory_space=pl.ANY)
    in_specs = [
        pl.BlockSpec((1, T, D), lambda b, t: (b, t, 0)),
        pl.BlockSpec((1, MIX_SUB, D), next_rows),
        _const_spec((1, D)),
        hbm,
        pl.BlockSpec((T, LANES), lambda b, t: (t, 0)),
        pl.BlockSpec((T, LANES), lambda b, t: (t, 0)),
        _const_spec(dmask.shape),
        _const_spec(qdec.shape),
        _const_spec(kdec.shape),
        _const_spec((1, RET_WIDTH)),
        _const_spec((1, GMLP_WIDTH)),
        _const_spec((1, GMLP_WIDTH)),
        _const_spec(w_s.shape),
        _const_spec(bias.shape),
        hbm,
        up_spec,
        up_spec,
        down_spec,
    ]
    return pl.pallas_call(
        functools.partial(_mixer_kernel, tile_decay=tile_decay),
        grid=(B, S // T),
        in_specs=in_specs,
        out_specs=[pl.BlockSpec((1, T, D), lambda b, t: (b, t, 0)), up_spec, up_spec, down_spec],
        out_shape=[jax.ShapeDtypeStruct((B, S, D), F32),
                   jax.ShapeDtypeStruct(w_gate.shape, BF16),
                   jax.ShapeDtypeStruct(w_up.shape, BF16),
                   jax.ShapeDtypeStruct(w_down.shape, BF16)],
        scratch_shapes=[
            pltpu.VMEM((HEADS, HEAD_DIM, HEAD_DIM), F32),
            pltpu.VMEM((MIX_SUB, w_in.shape[1]), F32),
            pltpu.VMEM(w_in.shape, BF16),
            pltpu.VMEM(w_out.shape, BF16),
            pltpu.VMEM((WEIGHT_STAGE_SLOTS, WEIGHT_STAGE_ROWS, w_in.shape[1]), F32),
            pltpu.VMEM((WEIGHT_STAGE_SLOTS, WEIGHT_STAGE_ROWS, w_out.shape[1]), F32),
            pltpu.SemaphoreType.DMA((WEIGHT_STAGE_SLOTS,)),
        ],
        compiler_params=pltpu.CompilerParams(
            dimension_semantics=("arbitrary", "arbitrary"),
            vmem_limit_bytes=V7X_VMEM_LIMIT_BYTES),
        name="token_mixer",
    )(x, x, row(norm1_g), w_in, cos, sin, dmask, qdec, kdec,
      row(ret_gn_g), row(gmlp_ln_g), row(gmlp_ln_b), w_s, bias, w_out, w_gate, w_up, w_down)


def _ffn(x, norm2_g, w_gate, w_up, w_down, final_g):
    N, D = x.shape
    T = FFN_TILE
    row = lambda a: a.reshape(1, -1)
    return pl.pallas_call(
        _ffn_kernel,
        grid=(N // T,),
        in_specs=[
            pl.BlockSpec((T, D), lambda i: (i, 0)),
            _const_spec((1, D)),
            _const_spec(w_gate.shape),
            _const_spec(w_up.shape),
            _const_spec(w_down.shape),
            _const_spec((1, D)),
        ],
        out_specs=pl.BlockSpec((T, D), lambda i: (i, 0)),
        out_shape=jax.ShapeDtypeStruct((N, D), F32),
        compiler_params=pltpu.CompilerParams(
            dimension_semantics=("arbitrary",),
            vmem_limit_bytes=V7X_VMEM_LIMIT_BYTES),
        name="swiglu_ffn",
    )(x, row(norm2_g), w_gate, w_up, w_down, row(final_g))


def kernel(x, norm1_g, w_in, ret_gn_g, gmlp_ln_g, gmlp_ln_b, w_s, b_s, w_out, norm2_g,
           w_ffn_gate, w_ffn_up, w_ffn_down, final_g):
    B, S, D = x.shape
    assert w_in.shape[0] == 1, "the FFN kernel fuses the final norm, so exactly one layer is supported"
    assert D == D_MODEL and S % SEQ_TILE == 0 and (B * S) % FFN_TILE == 0
    x1, w_gate, w_up, w_down = _mixer(
        x, norm1_g[0], w_in[0], ret_gn_g[0], gmlp_ln_g[0], gmlp_ln_b[0], w_s[0], b_s[0], w_out[0],
        w_ffn_gate[0], w_ffn_up[0], w_ffn_down[0])
    out = _ffn(x1.reshape(B * S, D), norm2_g[0], w_gate, w_up, w_down, final_g)
    return out.reshape(B, S, D)
```

```python
import functools

import numpy as np
import jax
import jax.numpy as jnp
from jax import lax
from jax.experimental import pallas as pl
from jax.experimental.pallas import tpu as pltpu

D_MODEL = 1024
CHUNK = 64
RET_WIDTH = D_MODEL // 2
HEAD_DIM = 128
HEADS = RET_WIDTH // HEAD_DIM
GMLP_WIDTH = D_MODEL - RET_WIDTH
GROUP_DIM = 128
GROUPS = GMLP_WIDTH // GROUP_DIM
GMLP_BLOCK = 128
D_FF = 2816
ROPE_THETA = 10000.0
EPS = 1e-6

LANES = 128
V7X_VMEM_LIMIT_BYTES = 56 * 1024 * 1024

SEQ_TILE = 1024
MIX_SUB = 256
FFN_TILE = 2048
FFN_SUB = 256
FF_CHUNK = 256
WEIGHT_STAGE_ROWS = 128
WEIGHT_STAGE_SLOTS = 4
FFN_DOWN_ROWS = 128
BF16_SUBLANES = 16

F32 = jnp.float32
BF16 = jnp.bfloat16


def _rms_scale(x):
    return x * lax.rsqrt(jnp.mean(x * x, axis=-1, keepdims=True) + EPS)


def _dot(a, b):
    return jnp.dot(a, b, preferred_element_type=F32)


def _dot_nt(a, b):
    return lax.dot_general(a, b, (((1,), (1,)), ((), ())), preferred_element_type=F32)


def _dot_tn(a, b):
    return lax.dot_general(a, b, (((0,), (0,)), ((), ())), preferred_element_type=F32)


def _load_weight_as_bf16(w_hbm, w_vmem, stage, sem):
    slots, rows = stage.shape[0], stage.shape[1]
    n_chunks = w_hbm.shape[0] // rows
    assert n_chunks * rows == w_hbm.shape[0] and stage.shape[2] == w_hbm.shape[1]

    def copy(i):
        return pltpu.make_async_copy(w_hbm.at[pl.ds(i * rows, rows), :], stage.at[i % slots], sem.at[i % slots])

    for i in range(min(slots - 1, n_chunks)):
        copy(i).start()
    for i in range(n_chunks):
        if i + slots - 1 < n_chunks:
            copy(i + slots - 1).start()
        copy(i).wait()
        w_vmem[pl.ds(i * rows, rows), :] = stage[i % slots].astype(BF16)


def _mixer_kernel(x_ref, g1_ref, w_in_hbm, cos_ref, sin_ref, dmask_ref, qdec_ref, kdec_ref,
                  gn_g_ref, ln_g_ref, ln_b_ref, ws_ref, bs_ref, w_out_hbm,
                  ffn_gate_ref, ffn_up_ref, ffn_down_ref,
                  o_ref, ffn_gate_bf_ref, ffn_up_bf_ref, ffn_down_bf_ref,
                  state_ref, w_in_ref, w_out_ref, stage_in, stage_out, sem, *, tile_decay):
    @pl.when((pl.program_id(0) == 0) & (pl.program_id(1) == 0))
    def _():
        _load_weight_as_bf16(w_in_hbm, w_in_ref, stage_in, sem)
        _load_weight_as_bf16(w_out_hbm, w_out_ref, stage_out, sem)

    ffn_gate_bf_ref[...] = ffn_gate_ref[...].astype(BF16)
    ffn_up_bf_ref[...] = ffn_up_ref[...].astype(BF16)
    ffn_down_bf_ref[...] = ffn_down_ref[...].astype(BF16)

    @pl.when(pl.program_id(1) == 0)
    def _():
        state_ref[...] = jnp.zeros_like(state_ref)

    n_sub = x_ref.shape[1] // MIX_SUB

    def rows(s):
        return pl.ds(s * MIX_SUB, MIX_SUB)

    def norm(s):
        return (_rms_scale(x_ref[0, rows(s), :]) * g1_ref[...]).astype(BF16)

    def proj(h, i):
        return _dot(h, w_in_ref[:, i * RET_WIDTH:(i + 1) * RET_WIDTH])

    def retention_head(s, hd, q, k, v):
        sl = slice(hd * HEAD_DIM, (hd + 1) * HEAD_DIM)
        cos = cos_ref[rows(s), :]
        sin = sin_ref[rows(s), :]
        qh, kh, vh = q[:, sl], k[:, sl], v[:, sl]
        qh = (qh * cos + pltpu.roll(qh, HEAD_DIM // 2, 1) * sin).astype(BF16)
        kh = (kh * cos + pltpu.roll(kh, HEAD_DIM // 2, 1) * sin).astype(BF16)
        scores = (_dot_nt(qh, kh) * dmask_ref[hd]).astype(BF16)
        inner = _dot(scores, vh.astype(BF16))
        state = state_ref[hd]
        cross = _dot(qh, state.astype(BF16)) * qdec_ref[hd]
        kv = _dot_tn(kh, (vh * kdec_ref[hd]).astype(BF16))
        state_ref[hd] = tile_decay[hd] * state + kv
        y = inner + cross
        mu = jnp.mean(y, axis=-1, keepdims=True)
        yc = y - mu
        var = jnp.mean(yc * yc, axis=-1, keepdims=True)
        return yc * lax.rsqrt(var + EPS)

    row = lax.broadcasted_iota(jnp.int32, (GMLP_BLOCK, GMLP_BLOCK), 0) // CHUNK
    col = lax.broadcasted_iota(jnp.int32, (GMLP_BLOCK, GMLP_BLOCK), 1) // CHUNK
    causal = row >= col

    h = norm(0)
    p = [proj(h, i) for i in range(6)]
    for s in range(n_sub):
        q, k, v, g, u, vg = p
        more = s + 1 < n_sub
        p_next = []
        if more:
            h = norm(s + 1)

        def next_proj(i):
            if more:
                p_next.append(proj(h, i))

        ret_heads = []
        for hd in range(HEADS):
            ret_heads.append(retention_head(s, hd, q, k, v))
            next_proj(hd)
        ret = jnp.concatenate(ret_heads, axis=-1) * gn_g_ref[...] * (g * jax.nn.sigmoid(g))

        ug = jax.nn.gelu(u)
        vgg = jax.nn.gelu(vg)
        mu = jnp.mean(vgg, axis=-1, keepdims=True)
        vc = vgg - mu
        var = jnp.mean(vc * vc, axis=-1, keepdims=True)
        vn = (vc * lax.rsqrt(var + EPS) * ln_g_ref[...] + ln_b_ref[...]).astype(BF16)
        next_proj(4)
        mixed_blocks = []
        for m in range(MIX_SUB // GMLP_BLOCK):
            blk = slice(m * GMLP_BLOCK, (m + 1) * GMLP_BLOCK)
            groups = []
            for gi in range(GROUPS):
                w = jnp.where(causal, ws_ref[gi], 0.0).astype(BF16)
                groups.append(_dot(w, vn[blk, gi * GROUP_DIM:(gi + 1) * GROUP_DIM]))
            mixed_blocks.append(jnp.concatenate(groups, axis=-1) + bs_ref[...])
        gm = ug * jnp.concatenate(mixed_blocks, axis=0)
        next_proj(5)

        mix = jnp.concatenate([ret, gm], axis=-1).astype(BF16)
        o_ref[0, rows(s), :] = x_ref[0, rows(s), :] + _dot(mix, w_out_ref[...])
        p = p_next


def _ffn_kernel(x_ref, g2_ref, wg_ref, wu_ref, wd_ref, gf_ref, o_ref):
    n_chunks = D_FF // FF_CHUNK
    n_sub = x_ref.shape[0] // FFN_SUB

    def rows(s):
        return pl.ds(s * FFN_SUB, FFN_SUB)

    def prologue(s):
        return (_rms_scale(x_ref[rows(s), :]) * g2_ref[...]).astype(BF16)

    def epilogue(s, acc):
        o_ref[rows(s), :] = _rms_scale(x_ref[rows(s), :] + acc) * gf_ref[...]

    def gate_up(h, c):
        sl = slice(c * FF_CHUNK, (c + 1) * FF_CHUNK)
        return _dot(h, wg_ref[:, sl]), _dot(h, wu_ref[:, sl])

    h = prologue(0)
    done = None
    for s in range(n_sub):
        h_next = None
        acc = None
        gate, up = gate_up(h, 0)
        for c in range(n_chunks):
            act = (gate * jax.nn.sigmoid(gate) * up).astype(BF16)
            if c + 1 < n_chunks:
                gate, up = gate_up(h, c + 1)
            part = _dot(act, wd_ref[c * FF_CHUNK:(c + 1) * FF_CHUNK, :])
            acc = part if acc is None else acc + part
            if c == 0 and s + 1 < n_sub:
                h_next = prologue(s + 1)
            if c == 1 and done is not None:
                epilogue(*done)
        done = (s, acc)
        h = h_next
    epilogue(*done)


def _const_spec(shape):
    zeros = (0,) * len(shape)
    return pl.BlockSpec(shape, lambda *_: zeros, pipeline_mode=pl.Buffered(1))


def _decay_tables(tile):
    log_gamma = np.log(1.0 - np.power(2.0, -5.0 - np.arange(HEADS, dtype=np.float64)))
    idx = np.arange(tile, dtype=np.float64)
    diff = idx[:, None] - idx[None, :]
    same = (np.arange(tile)[:, None] // CHUNK) == (np.arange(tile)[None, :] // CHUNK)
    earlier = (np.arange(tile)[:, None] // CHUNK) > (np.arange(tile)[None, :] // CHUNK)
    expo = np.where(same, np.abs(diff), diff)
    scale = HEAD_DIM ** -0.5
    dmask = np.where(same | earlier, np.exp(log_gamma[:, None, None] * expo[None]), 0.0) * scale
    qdec = np.exp(log_gamma[:, None] * (idx + 1.0)[None, :]) * scale
    kdec = np.exp(log_gamma[:, None] * (tile - 1.0 - idx)[None, :])
    tile_decay = tuple(float(v) for v in np.exp(log_gamma * tile))
    bcast = lambda a: np.broadcast_to(a[:, :, None], (HEADS, tile, LANES))
    return (jnp.asarray(dmask, F32), jnp.asarray(bcast(qdec), F32),
            jnp.asarray(bcast(kdec), F32), tile_decay)


def _rope_tables(seq):
    inv = ROPE_THETA ** (-np.arange(0, HEAD_DIM, 2, dtype=np.float64) / HEAD_DIM)
    ang = np.arange(seq, dtype=np.float64)[:, None] * inv[None, :]
    cos = np.concatenate([np.cos(ang), np.cos(ang)], axis=-1)
    sin = np.concatenate([-np.sin(ang), np.sin(ang)], axis=-1)
    return jnp.asarray(cos, F32), jnp.asarray(sin, F32)


def _mixer(x, norm1_g, w_in, ret_gn_g, gmlp_ln_g, gmlp_ln_b, w_s, b_s, w_out, w_gate, w_up, w_down):
    B, S, D = x.shape
    T = SEQ_TILE
    steps = B * (S // T)
    cos, sin = _rope_tables(S)
    dmask, qdec, kdec, tile_decay = _decay_tables(MIX_SUB)
    bias = jnp.repeat(b_s.T, GROUP_DIM, axis=1)

    up_rows = D // steps
    down_slices = D_FF // FFN_DOWN_ROWS
    assert up_rows * steps == D and up_rows % BF16_SUBLANES == 0
    assert down_slices * FFN_DOWN_ROWS == D_FF and down_slices <= steps
    step_of = lambda b, t: b * (S // T) + t
    up_spec = pl.BlockSpec((up_rows, D_FF), lambda b, t: (step_of(b, t), 0))
    down_spec = pl.BlockSpec((FFN_DOWN_ROWS, D),
                             lambda b, t: (jnp.minimum(step_of(b, t), down_slices - 1), 0))

    row = lambda a: a.reshape(1, -1)
    hbm = pl.BlockSpec(memory_space=pl.ANY)
    in_specs = [
        pl.BlockSpec((1, T, D), lambda b, t: (b, t, 0)),
        _const_spec((1, D)),
        hbm,
        pl.BlockSpec((T, LANES), lambda b, t: (t, 0)),
        pl.BlockSpec((T, LANES), lambda b, t: (t, 0)),
        _const_spec(dmask.shape),
        _const_spec(qdec.shape),
        _const_spec(kdec.shape),
        _const_spec((1, RET_WIDTH)),
        _const_spec((1, GMLP_WIDTH)),
        _const_spec((1, GMLP_WIDTH)),
        _const_spec(w_s.shape),
        _const_spec(bias.shape),
        hbm,
        up_spec,
        up_spec,
        down_spec,
    ]
    return pl.pallas_call(
        functools.partial(_mixer_kernel, tile_decay=tile_decay),
        grid=(B, S // T),
        in_specs=in_specs,
        out_specs=[pl.BlockSpec((1, T, D), lambda b, t: (b, t, 0)), up_spec, up_spec, down_spec],
        out_shape=[jax.ShapeDtypeStruct((B, S, D), F32),
                   jax.ShapeDtypeStruct(w_gate.shape, BF16),
                   jax.ShapeDtypeStruct(w_up.shape, BF16),
                   jax.ShapeDtypeStruct(w_down.shape, BF16)],
        scratch_shapes=[
            pltpu.VMEM((HEADS, HEAD_DIM, HEAD_DIM), F32),
            pltpu.VMEM(w_in.shape, BF16),
            pltpu.VMEM(w_out.shape, BF16),
            pltpu.VMEM((WEIGHT_STAGE_SLOTS, WEIGHT_STAGE_ROWS, w_in.shape[1]), F32),
            pltpu.VMEM((WEIGHT_STAGE_SLOTS, WEIGHT_STAGE_ROWS, w_out.shape[1]), F32),
            pltpu.SemaphoreType.DMA((WEIGHT_STAGE_SLOTS,)),
        ],
        compiler_params=pltpu.CompilerParams(
            dimension_semantics=("arbitrary", "arbitrary"),
            vmem_limit_bytes=V7X_VMEM_LIMIT_BYTES),
        name="token_mixer",
    )(x, row(norm1_g), w_in, cos, sin, dmask, qdec, kdec,
      row(ret_gn_g), row(gmlp_ln_g), row(gmlp_ln_b), w_s, bias, w_out, w_gate, w_up, w_down)


def _ffn(x, norm2_g, w_gate, w_up, w_down, final_g):
    N, D = x.shape
    T = FFN_TILE
    row = lambda a: a.reshape(1, -1)
    return pl.pallas_call(
        _ffn_kernel,
        grid=(N // T,),
        in_specs=[
            pl.BlockSpec((T, D), lambda i: (i, 0)),
            _const_spec((1, D)),
            _const_spec(w_gate.shape),
            _const_spec(w_up.shape),
            _const_spec(w_down.shape),
            _const_spec((1, D)),
        ],
        out_specs=pl.BlockSpec((T, D), lambda i: (i, 0)),
        out_shape=jax.ShapeDtypeStruct((N, D), F32),
        compiler_params=pltpu.CompilerParams(
            dimension_semantics=("arbitrary",),
            vmem_limit_bytes=V7X_VMEM_LIMIT_BYTES),
        name="swiglu_ffn",
    )(x, row(norm2_g), w_gate, w_up, w_down, row(final_g))


def kernel(x, norm1_g, w_in, ret_gn_g, gmlp_ln_g, gmlp_ln_b, w_s, b_s, w_out, norm2_g,
           w_ffn_gate, w_ffn_up, w_ffn_down, final_g):
    B, S, D = x.shape
    assert w_in.shape[0] == 1, "the FFN kernel fuses the final norm, so exactly one layer is supported"
    assert D == D_MODEL and S % SEQ_TILE == 0 and (B * S) % FFN_TILE == 0
    x1, w_gate, w_up, w_down = _mixer(
        x, norm1_g[0], w_in[0], ret_gn_g[0], gmlp_ln_g[0], gmlp_ln_b[0], w_s[0], b_s[0], w_out[0],
        w_ffn_gate[0], w_ffn_up[0], w_ffn_down[0])
    out = _ffn(x1.reshape(B * S, D), norm2_g[0], w_gate, w_up, w_down, final_g)
    return out.reshape(B, S, D)
```

```python
import functools

import numpy as np
import jax
import jax.numpy as jnp
from jax import lax
from jax.experimental import pallas as pl
from jax.experimental.pallas import tpu as pltpu

D_MODEL = 1024
CHUNK = 64
RET_WIDTH = D_MODEL // 2
HEAD_DIM = 128
HEADS = RET_WIDTH // HEAD_DIM
GMLP_WIDTH = D_MODEL - RET_WIDTH
GROUP_DIM = 128
GROUPS = GMLP_WIDTH // GROUP_DIM
GMLP_BLOCK = 128
D_FF = 2816
ROPE_THETA = 10000.0
EPS = 1e-6

LANES = 128
V7X_VMEM_LIMIT_BYTES = 56 * 1024 * 1024

SEQ_TILE = 1024
MIX_SUB = 256
FFN_TILE = 512
FFN_SUB = 256
FF_CHUNK = 256
WEIGHT_STAGE_ROWS = 128
WEIGHT_STAGE_SLOTS = 4
FFN_DOWN_ROWS = 128
BF16_SUBLANES = 16

F32 = jnp.float32
BF16 = jnp.bfloat16


def _rms_scale(x):
    return x * lax.rsqrt(jnp.mean(x * x, axis=-1, keepdims=True) + EPS)


def _dot(a, b):
    return jnp.dot(a, b, preferred_element_type=F32)


def _dot_nt(a, b):
    return lax.dot_general(a, b, (((1,), (1,)), ((), ())), preferred_element_type=F32)


def _dot_tn(a, b):
    return lax.dot_general(a, b, (((0,), (0,)), ((), ())), preferred_element_type=F32)


def _load_weight_as_bf16(w_hbm, w_vmem, stage, sem):
    slots, rows = stage.shape[0], stage.shape[1]
    n_chunks = w_hbm.shape[0] // rows
    assert n_chunks * rows == w_hbm.shape[0] and stage.shape[2] == w_hbm.shape[1]

    def copy(i):
        return pltpu.make_async_copy(w_hbm.at[pl.ds(i * rows, rows), :], stage.at[i % slots], sem.at[i % slots])

    for i in range(min(slots - 1, n_chunks)):
        copy(i).start()
    for i in range(n_chunks):
        if i + slots - 1 < n_chunks:
            copy(i + slots - 1).start()
        copy(i).wait()
        w_vmem[pl.ds(i * rows, rows), :] = stage[i % slots].astype(BF16)


def _mixer_kernel(x_ref, g1_ref, w_in_hbm, cos_ref, sin_ref, dmask_ref, qdec_ref, kdec_ref,
                  gn_g_ref, ln_g_ref, ln_b_ref, ws_ref, bs_ref, w_out_hbm,
                  ffn_gate_ref, ffn_up_ref, ffn_down_ref,
                  o_ref, ffn_gate_bf_ref, ffn_up_bf_ref, ffn_down_bf_ref,
                  state_ref, w_in_ref, w_out_ref, stage_in, stage_out, sem, *, tile_decay):
    @pl.when((pl.program_id(0) == 0) & (pl.program_id(1) == 0))
    def _():
        _load_weight_as_bf16(w_in_hbm, w_in_ref, stage_in, sem)
        _load_weight_as_bf16(w_out_hbm, w_out_ref, stage_out, sem)

    ffn_gate_bf_ref[...] = ffn_gate_ref[...].astype(BF16)
    ffn_up_bf_ref[...] = ffn_up_ref[...].astype(BF16)
    ffn_down_bf_ref[...] = ffn_down_ref[...].astype(BF16)

    @pl.when(pl.program_id(1) == 0)
    def _():
        state_ref[...] = jnp.zeros_like(state_ref)

    n_sub = x_ref.shape[1] // MIX_SUB

    def rows(s):
        return pl.ds(s * MIX_SUB, MIX_SUB)

    def norm(s):
        return (_rms_scale(x_ref[0, rows(s), :]) * g1_ref[...]).astype(BF16)

    def proj(h, i):
        return _dot(h, w_in_ref[:, i * RET_WIDTH:(i + 1) * RET_WIDTH])

    def retention_head(s, hd, q, k, v):
        sl = slice(hd * HEAD_DIM, (hd + 1) * HEAD_DIM)
        cos = cos_ref[rows(s), :]
        sin = sin_ref[rows(s), :]
        qh, kh, vh = q[:, sl], k[:, sl], v[:, sl]
        qh = (qh * cos + pltpu.roll(qh, HEAD_DIM // 2, 1) * sin).astype(BF16)
        kh = (kh * cos + pltpu.roll(kh, HEAD_DIM // 2, 1) * sin).astype(BF16)
        scores = (_dot_nt(qh, kh) * dmask_ref[hd]).astype(BF16)
        inner = _dot(scores, vh.astype(BF16))
        state = state_ref[hd]
        cross = _dot(qh, state.astype(BF16)) * qdec_ref[hd]
        kv = _dot_tn(kh, (vh * kdec_ref[hd]).astype(BF16))
        state_ref[hd] = tile_decay[hd] * state + kv
        y = inner + cross
        mu = jnp.mean(y, axis=-1, keepdims=True)
        yc = y - mu
        var = jnp.mean(yc * yc, axis=-1, keepdims=True)
        return yc * lax.rsqrt(var + EPS)

    row = lax.broadcasted_iota(jnp.int32, (GMLP_BLOCK, GMLP_BLOCK), 0) // CHUNK
    col = lax.broadcasted_iota(jnp.int32, (GMLP_BLOCK, GMLP_BLOCK), 1) // CHUNK
    causal = row >= col

    h = norm(0)
    p = [proj(h, i) for i in range(6)]
    for s in range(n_sub):
        q, k, v, g, u, vg = p
        more = s + 1 < n_sub
        p_next = []
        if more:
            h = norm(s + 1)

        def next_proj(i):
            if more:
                p_next.append(proj(h, i))

        ret_heads = []
        for hd in range(HEADS):
            ret_heads.append(retention_head(s, hd, q, k, v))
            next_proj(hd)
        ret = jnp.concatenate(ret_heads, axis=-1) * gn_g_ref[...] * (g * jax.nn.sigmoid(g))

        ug = jax.nn.gelu(u)
        vgg = jax.nn.gelu(vg)
        mu = jnp.mean(vgg, axis=-1, keepdims=True)
        vc = vgg - mu
        var = jnp.mean(vc * vc, axis=-1, keepdims=True)
        vn = (vc * lax.rsqrt(var + EPS) * ln_g_ref[...] + ln_b_ref[...]).astype(BF16)
        next_proj(4)
        mixed_blocks = []
        for m in range(MIX_SUB // GMLP_BLOCK):
            blk = slice(m * GMLP_BLOCK, (m + 1) * GMLP_BLOCK)
            groups = []
            for gi in range(GROUPS):
                w = jnp.where(causal, ws_ref[gi], 0.0).astype(BF16)
                groups.append(_dot(w, vn[blk, gi * GROUP_DIM:(gi + 1) * GROUP_DIM]))
            mixed_blocks.append(jnp.concatenate(groups, axis=-1) + bs_ref[...])
        gm = ug * jnp.concatenate(mixed_blocks, axis=0)
        next_proj(5)

        mix = jnp.concatenate([ret, gm], axis=-1).astype(BF16)
        o_ref[0, rows(s), :] = x_ref[0, rows(s), :] + _dot(mix, w_out_ref[...])
        p = p_next


def _ffn_kernel(x_ref, g2_ref, wg_ref, wu_ref, wd_ref, gf_ref, o_ref):
    n_chunks = D_FF // FF_CHUNK
    n_sub = x_ref.shape[0] // FFN_SUB

    def rows(s):
        return pl.ds(s * FFN_SUB, FFN_SUB)

    def prologue(s):
        return (_rms_scale(x_ref[rows(s), :]) * g2_ref[...]).astype(BF16)

    def epilogue(s, acc):
        o_ref[rows(s), :] = _rms_scale(x_ref[rows(s), :] + acc) * gf_ref[...]

    def gate_up(h, c):
        sl = slice(c * FF_CHUNK, (c + 1) * FF_CHUNK)
        return _dot(h, wg_ref[:, sl]), _dot(h, wu_ref[:, sl])

    h = prologue(0)
    done = None
    for s in range(n_sub):
        h_next = None
        acc = None
        gate, up = gate_up(h, 0)
        for c in range(n_chunks):
            act = (gate * jax.nn.sigmoid(gate) * up).astype(BF16)
            if c + 1 < n_chunks:
                gate, up = gate_up(h, c + 1)
            part = _dot(act, wd_ref[c * FF_CHUNK:(c + 1) * FF_CHUNK, :])
            acc = part if acc is None else acc + part
            if c == 0 and s + 1 < n_sub:
                h_next = prologue(s + 1)
            if c == 1 and done is not None:
                epilogue(*done)
        done = (s, acc)
        h = h_next
    epilogue(*done)


def _const_spec(shape):
    zeros = (0,) * len(shape)
    return pl.BlockSpec(shape, lambda *_: zeros, pipeline_mode=pl.Buffered(1))


def _decay_tables(tile):
    log_gamma = np.log(1.0 - np.power(2.0, -5.0 - np.arange(HEADS, dtype=np.float64)))
    idx = np.arange(tile, dtype=np.float64)
    diff = idx[:, None] - idx[None, :]
    same = (np.arange(tile)[:, None] // CHUNK) == (np.arange(tile)[None, :] // CHUNK)
    earlier = (np.arange(tile)[:, None] // CHUNK) > (np.arange(tile)[None, :] // CHUNK)
    expo = np.where(same, np.abs(diff), diff)
    scale = HEAD_DIM ** -0.5
    dmask = np.where(same | earlier, np.exp(log_gamma[:, None, None] * expo[None]), 0.0) * scale
    qdec = np.exp(log_gamma[:, None] * (idx + 1.0)[None, :]) * scale
    kdec = np.exp(log_gamma[:, None] * (tile - 1.0 - idx)[None, :])
    tile_decay = tuple(float(v) for v in np.exp(log_gamma * tile))
    bcast = lambda a: np.broadcast_to(a[:, :, None], (HEADS, tile, LANES))
    return (jnp.asarray(dmask, F32), jnp.asarray(bcast(qdec), F32),
            jnp.asarray(bcast(kdec), F32), tile_decay)


def _rope_tables(seq):
    inv = ROPE_THETA ** (-np.arange(0, HEAD_DIM, 2, dtype=np.float64) / HEAD_DIM)
    ang = np.arange(seq, dtype=np.float64)[:, None] * inv[None, :]
    cos = np.concatenate([np.cos(ang), np.cos(ang)], axis=-1)
    sin = np.concatenate([-np.sin(ang), np.sin(ang)], axis=-1)
    return jnp.asarray(cos, F32), jnp.asarray(sin, F32)


def _mixer(x, norm1_g, w_in, ret_gn_g, gmlp_ln_g, gmlp_ln_b, w_s, b_s, w_out, w_gate, w_up, w_down):
    B, S, D = x.shape
    T = SEQ_TILE
    steps = B * (S // T)
    cos, sin = _rope_tables(S)
    dmask, qdec, kdec, tile_decay = _decay_tables(MIX_SUB)
    bias = jnp.repeat(b_s.T, GROUP_DIM, axis=1)

    up_rows = D // steps
    down_slices = D_FF // FFN_DOWN_ROWS
    assert up_rows * steps == D and up_rows % BF16_SUBLANES == 0
    assert down_slices * FFN_DOWN_ROWS == D_FF and down_slices <= steps
    step_of = lambda b, t: b * (S // T) + t
    up_spec = pl.BlockSpec((up_rows, D_FF), lambda b, t: (step_of(b, t), 0))
    down_spec = pl.BlockSpec((FFN_DOWN_ROWS, D),
                             lambda b, t: (jnp.minimum(step_of(b, t), down_slices - 1), 0))

    row = lambda a: a.reshape(1, -1)
    hbm = pl.BlockSpec(memory_space=pl.ANY)
    in_specs = [
        pl.BlockSpec((1, T, D), lambda b, t: (b, t, 0)),
        _const_spec((1, D)),
        hbm,
        pl.BlockSpec((T, LANES), lambda b, t: (t, 0)),
        pl.BlockSpec((T, LANES), lambda b, t: (t, 0)),
        _const_spec(dmask.shape),
        _const_spec(qdec.shape),
        _const_spec(kdec.shape),
        _const_spec((1, RET_WIDTH)),
        _const_spec((1, GMLP_WIDTH)),
        _const_spec((1, GMLP_WIDTH)),
        _const_spec(w_s.shape),
        _const_spec(bias.shape),
        hbm,
        up_spec,
        up_spec,
        down_spec,
    ]
    return pl.pallas_call(
        functools.partial(_mixer_kernel, tile_decay=tile_decay),
        grid=(B, S // T),
        in_specs=in_specs,
        out_specs=[pl.BlockSpec((1, T, D), lambda b, t: (b, t, 0)), up_spec, up_spec, down_spec],
        out_shape=[jax.ShapeDtypeStruct((B, S, D), F32),
                   jax.ShapeDtypeStruct(w_gate.shape, BF16),
                   jax.ShapeDtypeStruct(w_up.shape, BF16),
                   jax.ShapeDtypeStruct(w_down.shape, BF16)],
        scratch_shapes=[
            pltpu.VMEM((HEADS, HEAD_DIM, HEAD_DIM), F32),
            pltpu.VMEM(w_in.shape, BF16),
            pltpu.VMEM(w_out.shape, BF16),
            pltpu.VMEM((WEIGHT_STAGE_SLOTS, WEIGHT_STAGE_ROWS, w_in.shape[1]), F32),
            pltpu.VMEM((WEIGHT_STAGE_SLOTS, WEIGHT_STAGE_ROWS, w_out.shape[1]), F32),
            pltpu.SemaphoreType.DMA((WEIGHT_STAGE_SLOTS,)),
        ],
        compiler_params=pltpu.CompilerParams(
            dimension_semantics=("arbitrary", "arbitrary"),
            vmem_limit_bytes=V7X_VMEM_LIMIT_BYTES),
        name="token_mixer",
    )(x, row(norm1_g), w_in, cos, sin, dmask, qdec, kdec,
      row(ret_gn_g), row(gmlp_ln_g), row(gmlp_ln_b), w_s, bias, w_out, w_gate, w_up, w_down)


def _ffn(x, norm2_g, w_gate, w_up, w_down, final_g):
    N, D = x.shape
    T = FFN_TILE
    row = lambda a: a.reshape(1, -1)
    return pl.pallas_call(
        _ffn_kernel,
        grid=(N // T,),
        in_specs=[
            pl.BlockSpec((T, D), lambda i: (i, 0)),
            _const_spec((1, D)),
            _const_spec(w_gate.shape),
            _const_spec(w_up.shape),
            _const_spec(w_down.shape),
            _const_spec((1, D)),
        ],
        out_specs=pl.BlockSpec((T, D), lambda i: (i, 0)),
        out_shape=jax.ShapeDtypeStruct((N, D), F32),
        compiler_params=pltpu.CompilerParams(
            dimension_semantics=("arbitrary",),
            vmem_limit_bytes=V7X_VMEM_LIMIT_BYTES),
        name="swiglu_ffn",
    )(x, row(norm2_g), w_gate, w_up, w_down, row(final_g))


def kernel(x, norm1_g, w_in, ret_gn_g, gmlp_ln_g, gmlp_ln_b, w_s, b_s, w_out, norm2_g,
           w_ffn_gate, w_ffn_up, w_ffn_down, final_g):
    B, S, D = x.shape
    assert w_in.shape[0] == 1, "the FFN kernel fuses the final norm, so exactly one layer is supported"
    assert D == D_MODEL and S % SEQ_TILE == 0 and (B * S) % FFN_TILE == 0
    x1, w_gate, w_up, w_down = _mixer(
        x, norm1_g[0], w_in[0], ret_gn_g[0], gmlp_ln_g[0], gmlp_ln_b[0], w_s[0], b_s[0], w_out[0],
        w_ffn_gate[0], w_ffn_up[0], w_ffn_down[0])
    out = _ffn(x1.reshape(B * S, D), norm2_g[0], w_gate, w_up, w_down, final_g)
    return out.reshape(B, S, D)
```

```python
import functools

import numpy as np
import jax
import jax.numpy as jnp
from jax import lax
from jax.experimental import pallas as pl
from jax.experimental.pallas import tpu as pltpu

D_MODEL = 1024
CHUNK = 64
RET_WIDTH = D_MODEL // 2
HEAD_DIM = 128
HEADS = RET_WIDTH // HEAD_DIM
GMLP_WIDTH = D_MODEL - RET_WIDTH
GROUP_DIM = 128
GROUPS = GMLP_WIDTH // GROUP_DIM
GMLP_BLOCK = 128
D_FF = 2816
ROPE_THETA = 10000.0
EPS = 1e-6

LANES = 128
V7X_VMEM_LIMIT_BYTES = 56 * 1024 * 1024

SEQ_TILE = 1024
MIX_SUB = 256
FFN_TILE = 1024
FFN_SUB = 512
FF_CHUNK = 256
WEIGHT_STAGE_ROWS = 128
WEIGHT_STAGE_SLOTS = 4
FFN_DOWN_ROWS = 128
BF16_SUBLANES = 16

F32 = jnp.float32
BF16 = jnp.bfloat16


def _rms_scale(x):
    return x * lax.rsqrt(jnp.mean(x * x, axis=-1, keepdims=True) + EPS)


def _dot(a, b):
    return jnp.dot(a, b, preferred_element_type=F32)


def _dot_nt(a, b):
    return lax.dot_general(a, b, (((1,), (1,)), ((), ())), preferred_element_type=F32)


def _dot_tn(a, b):
    return lax.dot_general(a, b, (((0,), (0,)), ((), ())), preferred_element_type=F32)


def _load_weight_as_bf16(w_hbm, w_vmem, stage, sem):
    slots, rows = stage.shape[0], stage.shape[1]
    n_chunks = w_hbm.shape[0] // rows
    assert n_chunks * rows == w_hbm.shape[0] and stage.shape[2] == w_hbm.shape[1]

    def copy(i):
        return pltpu.make_async_copy(w_hbm.at[pl.ds(i * rows, rows), :], stage.at[i % slots], sem.at[i % slots])

    for i in range(min(slots - 1, n_chunks)):
        copy(i).start()
    for i in range(n_chunks):
        if i + slots - 1 < n_chunks:
            copy(i + slots - 1).start()
        copy(i).wait()
        w_vmem[pl.ds(i * rows, rows), :] = stage[i % slots].astype(BF16)


def _mixer_kernel(x_ref, g1_ref, w_in_hbm, cos_ref, sin_ref, dmask_ref, qdec_ref, kdec_ref,
                  gn_g_ref, ln_g_ref, ln_b_ref, ws_ref, bs_ref, w_out_hbm,
                  ffn_gate_ref, ffn_up_ref, ffn_down_ref,
                  o_ref, ffn_gate_bf_ref, ffn_up_bf_ref, ffn_down_bf_ref,
                  state_ref, w_in_ref, w_out_ref, stage_in, stage_out, sem, *, tile_decay):
    @pl.when((pl.program_id(0) == 0) & (pl.program_id(1) == 0))
    def _():
        _load_weight_as_bf16(w_in_hbm, w_in_ref, stage_in, sem)
        _load_weight_as_bf16(w_out_hbm, w_out_ref, stage_out, sem)

    ffn_gate_bf_ref[...] = ffn_gate_ref[...].astype(BF16)
    ffn_up_bf_ref[...] = ffn_up_ref[...].astype(BF16)
    ffn_down_bf_ref[...] = ffn_down_ref[...].astype(BF16)

    @pl.when(pl.program_id(1) == 0)
    def _():
        state_ref[...] = jnp.zeros_like(state_ref)

    n_sub = x_ref.shape[1] // MIX_SUB

    def rows(s):
        return pl.ds(s * MIX_SUB, MIX_SUB)

    def norm(s):
        return (_rms_scale(x_ref[0, rows(s), :]) * g1_ref[...]).astype(BF16)

    def proj(h, i):
        return _dot(h, w_in_ref[:, i * RET_WIDTH:(i + 1) * RET_WIDTH])

    def retention_head(s, hd, q, k, v):
        sl = slice(hd * HEAD_DIM, (hd + 1) * HEAD_DIM)
        cos = cos_ref[rows(s), :]
        sin = sin_ref[rows(s), :]
        qh, kh, vh = q[:, sl], k[:, sl], v[:, sl]
        qh = (qh * cos + pltpu.roll(qh, HEAD_DIM // 2, 1) * sin).astype(BF16)
        kh = (kh * cos + pltpu.roll(kh, HEAD_DIM // 2, 1) * sin).astype(BF16)
        scores = (_dot_nt(qh, kh) * dmask_ref[hd]).astype(BF16)
        inner = _dot(scores, vh.astype(BF16))
        state = state_ref[hd]
        cross = _dot(qh, state.astype(BF16)) * qdec_ref[hd]
        kv = _dot_tn(kh, (vh * kdec_ref[hd]).astype(BF16))
        state_ref[hd] = tile_decay[hd] * state + kv
        y = inner + cross
        mu = jnp.mean(y, axis=-1, keepdims=True)
        yc = y - mu
        var = jnp.mean(yc * yc, axis=-1, keepdims=True)
        return yc * lax.rsqrt(var + EPS)

    row = lax.broadcasted_iota(jnp.int32, (GMLP_BLOCK, GMLP_BLOCK), 0) // CHUNK
    col = lax.broadcasted_iota(jnp.int32, (GMLP_BLOCK, GMLP_BLOCK), 1) // CHUNK
    causal = row >= col

    h = norm(0)
    q, k, v = (proj(h, i) for i in range(3))
    for s in range(n_sub):
        more = s + 1 < n_sub
        ret_heads = [retention_head(s, 0, q, k, v)]
        vg = proj(h, 5)
        ret_heads.append(retention_head(s, 1, q, k, v))
        u = proj(h, 4)
        vgg = jax.nn.gelu(vg)
        mu = jnp.mean(vgg, axis=-1, keepdims=True)
        vc = vgg - mu
        var = jnp.mean(vc * vc, axis=-1, keepdims=True)
        vn = (vc * lax.rsqrt(var + EPS) * ln_g_ref[...] + ln_b_ref[...]).astype(BF16)
        ug = jax.nn.gelu(u)
        ret_heads.append(retention_head(s, 2, q, k, v))
        g = proj(h, 3)
        if more:
            h = norm(s + 1)
        ret_heads.append(retention_head(s, 3, q, k, v))
        p_next = []

        def next_proj(i):
            if more:
                p_next.append(proj(h, i))

        next_proj(0)
        ret = jnp.concatenate(ret_heads, axis=-1) * gn_g_ref[...] * (g * jax.nn.sigmoid(g))
        next_proj(1)
        mixed_blocks = []
        for m in range(MIX_SUB // GMLP_BLOCK):
            blk = slice(m * GMLP_BLOCK, (m + 1) * GMLP_BLOCK)
            groups = []
            for gi in range(GROUPS):
                w = jnp.where(causal, ws_ref[gi], 0.0).astype(BF16)
                groups.append(_dot(w, vn[blk, gi * GROUP_DIM:(gi + 1) * GROUP_DIM]))
            mixed_blocks.append(jnp.concatenate(groups, axis=-1) + bs_ref[...])
        gm = ug * jnp.concatenate(mixed_blocks, axis=0)
        next_proj(2)

        mix = jnp.concatenate([ret, gm], axis=-1).astype(BF16)
        o_ref[0, rows(s), :] = x_ref[0, rows(s), :] + _dot(mix, w_out_ref[...])
        if more:
            q, k, v = p_next


def _ffn_kernel(x_ref, g2_ref, wg_ref, wu_ref, wd_ref, gf_ref, o_ref):
    n_chunks = D_FF // FF_CHUNK
    n_sub = x_ref.shape[0] // FFN_SUB

    def rows(s):
        return pl.ds(s * FFN_SUB, FFN_SUB)

    def prologue(s):
        return (_rms_scale(x_ref[rows(s), :]) * g2_ref[...]).astype(BF16)

    def epilogue(s, acc):
        o_ref[rows(s), :] = _rms_scale(x_ref[rows(s), :] + acc) * gf_ref[...]

    def gate_up(h, c):
        sl = slice(c * FF_CHUNK, (c + 1) * FF_CHUNK)
        return _dot(h, wg_ref[:, sl]), _dot(h, wu_ref[:, sl])

    h = prologue(0)
    done = None
    for s in range(n_sub):
        h_next = None
        acc = None
        gate, up = gate_up(h, 0)
        for c in range(n_chunks):
            act = (gate * jax.nn.sigmoid(gate) * up).astype(BF16)
            if c + 1 < n_chunks:
                gate, up = gate_up(h, c + 1)
            part = _dot(act, wd_ref[c * FF_CHUNK:(c + 1) * FF_CHUNK, :])
            acc = part if acc is None else acc + part
            if c == 0 and s + 1 < n_sub:
                h_next = prologue(s + 1)
            if c == 1 and done is not None:
                epilogue(*done)
        done = (s, acc)
        h = h_next
    epilogue(*done)


def _const_spec(shape):
    zeros = (0,) * len(shape)
    return pl.BlockSpec(shape, lambda *_: zeros, pipeline_mode=pl.Buffered(1))


def _decay_tables(tile):
    log_gamma = np.log(1.0 - np.power(2.0, -5.0 - np.arange(HEADS, dtype=np.float64)))
    idx = np.arange(tile, dtype=np.float64)
    diff = idx[:, None] - idx[None, :]
    same = (np.arange(tile)[:, None] // CHUNK) == (np.arange(tile)[None, :] // CHUNK)
    earlier = (np.arange(tile)[:, None] // CHUNK) > (np.arange(tile)[None, :] // CHUNK)
    expo = np.where(same, np.abs(diff), diff)
    scale = HEAD_DIM ** -0.5
    dmask = np.where(same | earlier, np.exp(log_gamma[:, None, None] * expo[None]), 0.0) * scale
    qdec = np.exp(log_gamma[:, None] * (idx + 1.0)[None, :]) * scale
    kdec = np.exp(log_gamma[:, None] * (tile - 1.0 - idx)[None, :])
    tile_decay = tuple(float(v) for v in np.exp(log_gamma * tile))
    bcast = lambda a: np.broadcast_to(a[:, :, None], (HEADS, tile, LANES))
    return (jnp.asarray(dmask, F32), jnp.asarray(bcast(qdec), F32),
            jnp.asarray(bcast(kdec), F32), tile_decay)


def _rope_tables(seq):
    inv = ROPE_THETA ** (-np.arange(0, HEAD_DIM, 2, dtype=np.float64) / HEAD_DIM)
    ang = np.arange(seq, dtype=np.float64)[:, None] * inv[None, :]
    cos = np.concatenate([np.cos(ang), np.cos(ang)], axis=-1)
    sin = np.concatenate([-np.sin(ang), np.sin(ang)], axis=-1)
    return jnp.asarray(cos, F32), jnp.asarray(sin, F32)


def _mixer(x, norm1_g, w_in, ret_gn_g, gmlp_ln_g, gmlp_ln_b, w_s, b_s, w_out, w_gate, w_up, w_down):
    B, S, D = x.shape
    T = SEQ_TILE
    steps = B * (S // T)
    cos, sin = _rope_tables(S)
    dmask, qdec, kdec, tile_decay = _decay_tables(MIX_SUB)
    bias = jnp.repeat(b_s.T, GROUP_DIM, axis=1)

    up_rows = D // steps
    down_slices = D_FF // FFN_DOWN_ROWS
    assert up_rows * steps == D and up_rows % BF16_SUBLANES == 0
    assert down_slices * FFN_DOWN_ROWS == D_FF and down_slices <= steps
    step_of = lambda b, t: b * (S // T) + t
    up_spec = pl.BlockSpec((up_rows, D_FF), lambda b, t: (step_of(b, t), 0))
    down_spec = pl.BlockSpec((FFN_DOWN_ROWS, D),
                             lambda b, t: (jnp.minimum(step_of(b, t), down_slices - 1), 0))

    row = lambda a: a.reshape(1, -1)
    hbm = pl.BlockSpec(memory_space=pl.ANY)
    in_specs = [
        pl.BlockSpec((1, T, D), lambda b, t: (b, t, 0)),
        _const_spec((1, D)),
        hbm,
        pl.BlockSpec((T, LANES), lambda b, t: (t, 0)),
        pl.BlockSpec((T, LANES), lambda b, t: (t, 0)),
        _const_spec(dmask.shape),
        _const_spec(qdec.shape),
        _const_spec(kdec.shape),
        _const_spec((1, RET_WIDTH)),
        _const_spec((1, GMLP_WIDTH)),
        _const_spec((1, GMLP_WIDTH)),
        _const_spec(w_s.shape),
        _const_spec(bias.shape),
        hbm,
        up_spec,
        up_spec,
        down_spec,
    ]
    return pl.pallas_call(
        functools.partial(_mixer_kernel, tile_decay=tile_decay),
        grid=(B, S // T),
        in_specs=in_specs,
        out_specs=[pl.BlockSpec((1, T, D), lambda b, t: (b, t, 0)), up_spec, up_spec, down_spec],
        out_shape=[jax.ShapeDtypeStruct((B, S, D), F32),
                   jax.ShapeDtypeStruct(w_gate.shape, BF16),
                   jax.ShapeDtypeStruct(w_up.shape, BF16),
                   jax.ShapeDtypeStruct(w_down.shape, BF16)],
        scratch_shapes=[
            pltpu.VMEM((HEADS, HEAD_DIM, HEAD_DIM), F32),
            pltpu.VMEM(w_in.shape, BF16),
            pltpu.VMEM(w_out.shape, BF16),
            pltpu.VMEM((WEIGHT_STAGE_SLOTS, WEIGHT_STAGE_ROWS, w_in.shape[1]), F32),
            pltpu.VMEM((WEIGHT_STAGE_SLOTS, WEIGHT_STAGE_ROWS, w_out.shape[1]), F32),
            pltpu.SemaphoreType.DMA((WEIGHT_STAGE_SLOTS,)),
        ],
        compiler_params=pltpu.CompilerParams(
            dimension_semantics=("arbitrary", "arbitrary"),
            vmem_limit_bytes=V7X_VMEM_LIMIT_BYTES),
        name="token_mixer",
    )(x, row(norm1_g), w_in, cos, sin, dmask, qdec, kdec,
      row(ret_gn_g), row(gmlp_ln_g), row(gmlp_ln_b), w_s, bias, w_out, w_gate, w_up, w_down)


def _ffn(x, norm2_g, w_gate, w_up, w_down, final_g):
    N, D = x.shape
    T = FFN_TILE
    row = lambda a: a.reshape(1, -1)
    return pl.pallas_call(
        _ffn_kernel,
        grid=(N // T,),
        in_specs=[
            pl.BlockSpec((T, D), lambda i: (i, 0)),
            _const_spec((1, D)),
            _const_spec(w_gate.shape),
            _const_spec(w_up.shape),
            _const_spec(w_down.shape),
            _const_spec((1, D)),
        ],
        out_specs=pl.BlockSpec((T, D), lambda i: (i, 0)),
        out_shape=jax.ShapeDtypeStruct((N, D), F32),
        compiler_params=pltpu.CompilerParams(
            dimension_semantics=("arbitrary",),
            vmem_limit_bytes=V7X_VMEM_LIMIT_BYTES),
        name="swiglu_ffn",
    )(x, row(norm2_g), w_gate, w_up, w_down, row(final_g))


def kernel(x, norm1_g, w_in, ret_gn_g, gmlp_ln_g, gmlp_ln_b, w_s, b_s, w_out, norm2_g,
           w_ffn_gate, w_ffn_up, w_ffn_down, final_g):
    B, S, D = x.shape
    assert w_in.shape[0] == 1, "the FFN kernel fuses the final norm, so exactly one layer is supported"
    assert D == D_MODEL and S % SEQ_TILE == 0 and (B * S) % FFN_TILE == 0
    x1, w_gate, w_up, w_down = _mixer(
        x, norm1_g[0], w_in[0], ret_gn_g[0], gmlp_ln_g[0], gmlp_ln_b[0], w_s[0], b_s[0], w_out[0],
        w_ffn_gate[0], w_ffn_up[0], w_ffn_down[0])
    out = _ffn(x1.reshape(B * S, D), norm2_g[0], w_gate, w_up, w_down, final_g)
    return out.reshape(B, S, D)
```

```python
import functools

import numpy as np
import jax
import jax.numpy as jnp
from jax import lax
from jax.experimental import pallas as pl
from jax.experimental.pallas import tpu as pltpu

D_MODEL = 1024
CHUNK = 64
RET_WIDTH = D_MODEL // 2
HEAD_DIM = 128
HEADS = RET_WIDTH // HEAD_DIM
GMLP_WIDTH = D_MODEL - RET_WIDTH
GROUP_DIM = 128
GROUPS = GMLP_WIDTH // GROUP_DIM
GMLP_BLOCK = 128
D_FF = 2816
ROPE_THETA = 10000.0
EPS = 1e-6

LANES = 128
V7X_VMEM_LIMIT_BYTES = 56 * 1024 * 1024

SEQ_TILE = 1024
MIX_SUB = 256
FFN_TILE = 1024
FFN_SUB = 256
FF_CHUNK = 256
WEIGHT_STAGE_ROWS = 128
WEIGHT_STAGE_SLOTS = 4
FFN_DOWN_ROWS = 128
BF16_SUBLANES = 16

F32 = jnp.float32
BF16 = jnp.bfloat16


def _rms_scale(x):
    return x * lax.rsqrt(jnp.mean(x * x, axis=-1, keepdims=True) + EPS)


_GELU_A = -2.0 * np.sqrt(2.0 / np.pi) * np.log2(np.e)
_GELU_B = _GELU_A * 0.044715


def _gelu(x):
    w = x * (_GELU_A + _GELU_B * (x * x))
    return x / (1.0 + jnp.exp2(w))


def _dot(a, b):
    return jnp.dot(a, b, preferred_element_type=F32)


def _dot_nt(a, b):
    return lax.dot_general(a, b, (((1,), (1,)), ((), ())), preferred_element_type=F32)


def _dot_tn(a, b):
    return lax.dot_general(a, b, (((0,), (0,)), ((), ())), preferred_element_type=F32)


def _load_weight_as_bf16(w_hbm, w_vmem, stage, sem):
    slots, rows = stage.shape[0], stage.shape[1]
    n_chunks = w_hbm.shape[0] // rows
    assert n_chunks * rows == w_hbm.shape[0] and stage.shape[2] == w_hbm.shape[1]

    def copy(i):
        return pltpu.make_async_copy(w_hbm.at[pl.ds(i * rows, rows), :], stage.at[i % slots], sem.at[i % slots])

    for i in range(min(slots - 1, n_chunks)):
        copy(i).start()
    for i in range(n_chunks):
        if i + slots - 1 < n_chunks:
            copy(i + slots - 1).start()
        copy(i).wait()
        w_vmem[pl.ds(i * rows, rows), :] = stage[i % slots].astype(BF16)


def _mixer_kernel(x_ref, g1_ref, w_in_hbm, cos_ref, sin_ref, dmask_ref, qdec_ref, kdec_ref,
                  gn_g_ref, ln_g_ref, ln_b_ref, ws_ref, bs_ref, w_out_hbm,
                  ffn_gate_ref, ffn_up_ref, ffn_down_ref,
                  o_ref, ffn_gate_bf_ref, ffn_up_bf_ref, ffn_down_bf_ref,
                  state_ref, w_in_ref, w_out_ref, stage_in, stage_out, sem, *, tile_decay):
    @pl.when((pl.program_id(0) == 0) & (pl.program_id(1) == 0))
    def _():
        _load_weight_as_bf16(w_in_hbm, w_in_ref, stage_in, sem)
        _load_weight_as_bf16(w_out_hbm, w_out_ref, stage_out, sem)

    ffn_gate_bf_ref[...] = ffn_gate_ref[...].astype(BF16)
    ffn_up_bf_ref[...] = ffn_up_ref[...].astype(BF16)
    ffn_down_bf_ref[...] = ffn_down_ref[...].astype(BF16)

    @pl.when(pl.program_id(1) == 0)
    def _():
        state_ref[...] = jnp.zeros_like(state_ref)

    n_sub = x_ref.shape[1] // MIX_SUB

    def rows(s):
        return pl.ds(s * MIX_SUB, MIX_SUB)

    def norm(s):
        return (_rms_scale(x_ref[0, rows(s), :]) * g1_ref[...]).astype(BF16)

    def proj(h, i):
        return _dot(h, w_in_ref[:, i * RET_WIDTH:(i + 1) * RET_WIDTH])

    def retention_head(s, hd, q, k, v):
        sl = slice(hd * HEAD_DIM, (hd + 1) * HEAD_DIM)
        cos = cos_ref[rows(s), :]
        sin = sin_ref[rows(s), :]
        qh, kh, vh = q[:, sl], k[:, sl], v[:, sl]
        qh = (qh * cos + pltpu.roll(qh, HEAD_DIM // 2, 1) * sin).astype(BF16)
        kh = (kh * cos + pltpu.roll(kh, HEAD_DIM // 2, 1) * sin).astype(BF16)
        scores = (_dot_nt(qh, kh) * dmask_ref[hd]).astype(BF16)
        inner = _dot(scores, vh.astype(BF16))
        state = state_ref[hd]
        cross = _dot(qh, state.astype(BF16)) * qdec_ref[hd]
        kv = _dot_tn(kh, (vh * kdec_ref[hd]).astype(BF16))
        state_ref[hd] = tile_decay[hd] * state + kv
        y = inner + cross
        mu = jnp.mean(y, axis=-1, keepdims=True)
        yc = y - mu
        var = jnp.mean(yc * yc, axis=-1, keepdims=True)
        return yc * lax.rsqrt(var + EPS)

    row = lax.broadcasted_iota(jnp.int32, (GMLP_BLOCK, GMLP_BLOCK), 0) // CHUNK
    col = lax.broadcasted_iota(jnp.int32, (GMLP_BLOCK, GMLP_BLOCK), 1) // CHUNK
    causal = row >= col

    h = norm(0)
    q, k, v = (proj(h, i) for i in range(3))
    for s in range(n_sub):
        more = s + 1 < n_sub
        ret_heads = [retention_head(s, 0, q, k, v)]
        vg = proj(h, 5)
        ret_heads.append(retention_head(s, 1, q, k, v))
        u = proj(h, 4)
        vgg = _gelu(vg)
        mu = jnp.mean(vgg, axis=-1, keepdims=True)
        vc = vgg - mu
        var = jnp.mean(vc * vc, axis=-1, keepdims=True)
        vn = (vc * lax.rsqrt(var + EPS) * ln_g_ref[...] + ln_b_ref[...]).astype(BF16)
        ug = _gelu(u)
        ret_heads.append(retention_head(s, 2, q, k, v))
        g = proj(h, 3)
        if more:
            h = norm(s + 1)
        ret_heads.append(retention_head(s, 3, q, k, v))
        p_next = []

        def next_proj(i):
            if more:
                p_next.append(proj(h, i))

        next_proj(0)
        ret = jnp.concatenate(ret_heads, axis=-1) * gn_g_ref[...] * (g * jax.nn.sigmoid(g))
        next_proj(1)
        mixed_blocks = []
        for m in range(MIX_SUB // GMLP_BLOCK):
            blk = slice(m * GMLP_BLOCK, (m + 1) * GMLP_BLOCK)
            groups = []
            for gi in range(GROUPS):
                w = jnp.where(causal, ws_ref[gi], 0.0).astype(BF16)
                groups.append(_dot(w, vn[blk, gi * GROUP_DIM:(gi + 1) * GROUP_DIM]))
            mixed_blocks.append(jnp.concatenate(groups, axis=-1) + bs_ref[...])
        gm = ug * jnp.concatenate(mixed_blocks, axis=0)
        next_proj(2)

        mix = jnp.concatenate([ret, gm], axis=-1).astype(BF16)
        o_ref[0, rows(s), :] = x_ref[0, rows(s), :] + _dot(mix, w_out_ref[...])
        if more:
            q, k, v = p_next


def _ffn_kernel(x_ref, g2_ref, wg_ref, wu_ref, wd_ref, gf_ref, o_ref):
    n_chunks = D_FF // FF_CHUNK
    n_sub = x_ref.shape[0] // FFN_SUB

    def rows(s):
        return pl.ds(s * FFN_SUB, FFN_SUB)

    def prologue(s):
        return (_rms_scale(x_ref[rows(s), :]) * g2_ref[...]).astype(BF16)

    def epilogue(s, acc):
        o_ref[rows(s), :] = _rms_scale(x_ref[rows(s), :] + acc) * gf_ref[...]

    def gate_up(h, c):
        sl = slice(c * FF_CHUNK, (c + 1) * FF_CHUNK)
        return _dot(h, wg_ref[:, sl]), _dot(h, wu_ref[:, sl])

    h = prologue(0)
    done = None
    for s in range(n_sub):
        h_next = None
        acc = None
        gate, up = gate_up(h, 0)
        for c in range(n_chunks):
            act = (gate * jax.nn.sigmoid(gate) * up).astype(BF16)
            if c + 1 < n_chunks:
                gate, up = gate_up(h, c + 1)
            part = _dot(act, wd_ref[c * FF_CHUNK:(c + 1) * FF_CHUNK, :])
            acc = part if acc is None else acc + part
            if c == 0 and s + 1 < n_sub:
                h_next = prologue(s + 1)
            if c == 1 and done is not None:
                epilogue(*done)
        done = (s, acc)
        h = h_next
    epilogue(*done)


def _const_spec(shape):
    zeros = (0,) * len(shape)
    return pl.BlockSpec(shape, lambda *_: zeros, pipeline_mode=pl.Buffered(1))


def _decay_tables(tile):
    log_gamma = np.log(1.0 - np.power(2.0, -5.0 - np.arange(HEADS, dtype=np.float64)))
    idx = np.arange(tile, dtype=np.float64)
    diff = idx[:, None] - idx[None, :]
    same = (np.arange(tile)[:, None] // CHUNK) == (np.arange(tile)[None, :] // CHUNK)
    earlier = (np.arange(tile)[:, None] // CHUNK) > (np.arange(tile)[None, :] // CHUNK)
    expo = np.where(same, np.abs(diff), diff)
    scale = HEAD_DIM ** -0.5
    dmask = np.where(same | earlier, np.exp(log_gamma[:, None, None] * expo[None]), 0.0) * scale
    qdec = np.exp(log_gamma[:, None] * (idx + 1.0)[None, :]) * scale
    kdec = np.exp(log_gamma[:, None] * (tile - 1.0 - idx)[None, :])
    tile_decay = tuple(float(v) for v in np.exp(log_gamma * tile))
    bcast = lambda a: np.broadcast_to(a[:, :, None], (HEADS, tile, LANES))
    return (jnp.asarray(dmask, F32), jnp.asarray(bcast(qdec), F32),
            jnp.asarray(bcast(kdec), F32), tile_decay)


def _rope_tables(seq):
    inv = ROPE_THETA ** (-np.arange(0, HEAD_DIM, 2, dtype=np.float64) / HEAD_DIM)
    ang = np.arange(seq, dtype=np.float64)[:, None] * inv[None, :]
    cos = np.concatenate([np.cos(ang), np.cos(ang)], axis=-1)
    sin = np.concatenate([-np.sin(ang), np.sin(ang)], axis=-1)
    return jnp.asarray(cos, F32), jnp.asarray(sin, F32)


def _mixer(x, norm1_g, w_in, ret_gn_g, gmlp_ln_g, gmlp_ln_b, w_s, b_s, w_out, w_gate, w_up, w_down):
    B, S, D = x.shape
    T = SEQ_TILE
    steps = B * (S // T)
    cos, sin = _rope_tables(S)
    dmask, qdec, kdec, tile_decay = _decay_tables(MIX_SUB)
    bias = jnp.repeat(b_s.T, GROUP_DIM, axis=1)

    up_rows = D // steps
    down_slices = D_FF // FFN_DOWN_ROWS
    assert up_rows * steps == D and up_rows % BF16_SUBLANES == 0
    assert down_slices * FFN_DOWN_ROWS == D_FF and down_slices <= steps
    step_of = lambda b, t: b * (S // T) + t
    up_spec = pl.BlockSpec((up_rows, D_FF), lambda b, t: (step_of(b, t), 0))
    down_spec = pl.BlockSpec((FFN_DOWN_ROWS, D),
                             lambda b, t: (jnp.minimum(step_of(b, t), down_slices - 1), 0))

    row = lambda a: a.reshape(1, -1)
    hbm = pl.BlockSpec(memory_space=pl.ANY)
    in_specs = [
        pl.BlockSpec((1, T, D), lambda b, t: (b, t, 0)),
        _const_spec((1, D)),
        hbm,
        pl.BlockSpec((T, LANES), lambda b, t: (t, 0)),
        pl.BlockSpec((T, LANES), lambda b, t: (t, 0)),
        _const_spec(dmask.shape),
        _const_spec(qdec.shape),
        _const_spec(kdec.shape),
        _const_spec((1, RET_WIDTH)),
        _const_spec((1, GMLP_WIDTH)),
        _const_spec((1, GMLP_WIDTH)),
        _const_spec(w_s.shape),
        _const_spec(bias.shape),
        hbm,
        up_spec,
        up_spec,
        down_spec,
    ]
    return pl.pallas_call(
        functools.partial(_mixer_kernel, tile_decay=tile_decay),
        grid=(B, S // T),
        in_specs=in_specs,
        out_specs=[pl.BlockSpec((1, T, D), lambda b, t: (b, t, 0)), up_spec, up_spec, down_spec],
        out_shape=[jax.ShapeDtypeStruct((B, S, D), F32),
                   jax.ShapeDtypeStruct(w_gate.shape, BF16),
                   jax.ShapeDtypeStruct(w_up.shape, BF16),
                   jax.ShapeDtypeStruct(w_down.shape, BF16)],
        scratch_shapes=[
            pltpu.VMEM((HEADS, HEAD_DIM, HEAD_DIM), F32),
            pltpu.VMEM(w_in.shape, BF16),
            pltpu.VMEM(w_out.shape, BF16),
            pltpu.VMEM((WEIGHT_STAGE_SLOTS, WEIGHT_STAGE_ROWS, w_in.shape[1]), F32),
            pltpu.VMEM((WEIGHT_STAGE_SLOTS, WEIGHT_STAGE_ROWS, w_out.shape[1]), F32),
            pltpu.SemaphoreType.DMA((WEIGHT_STAGE_SLOTS,)),
        ],
        compiler_params=pltpu.CompilerParams(
            dimension_semantics=("arbitrary", "arbitrary"),
            vmem_limit_bytes=V7X_VMEM_LIMIT_BYTES),
        name="token_mixer",
    )(x, row(norm1_g), w_in, cos, sin, dmask, qdec, kdec,
      row(ret_gn_g), row(gmlp_ln_g), row(gmlp_ln_b), w_s, bias, w_out, w_gate, w_up, w_down)


def _ffn(x, norm2_g, w_gate, w_up, w_down, final_g):
    N, D = x.shape
    T = FFN_TILE
    row = lambda a: a.reshape(1, -1)
    return pl.pallas_call(
        _ffn_kernel,
        grid=(N // T,),
        in_specs=[
            pl.BlockSpec((T, D), lambda i: (i, 0)),
            _const_spec((1, D)),
            _const_spec(w_gate.shape),
            _const_spec(w_up.shape),
            _const_spec(w_down.shape),
            _const_spec((1, D)),
        ],
        out_specs=pl.BlockSpec((T, D), lambda i: (i, 0)),
        out_shape=jax.ShapeDtypeStruct((N, D), F32),
        compiler_params=pltpu.CompilerParams(
            dimension_semantics=("arbitrary",),
            vmem_limit_bytes=V7X_VMEM_LIMIT_BYTES),
        name="swiglu_ffn",
    )(x, row(norm2_g), w_gate, w_up, w_down, row(final_g))


def kernel(x, norm1_g, w_in, ret_gn_g, gmlp_ln_g, gmlp_ln_b, w_s, b_s, w_out, norm2_g,
           w_ffn_gate, w_ffn_up, w_ffn_down, final_g):
    B, S, D = x.shape
    assert w_in.shape[0] == 1, "the FFN kernel fuses the final norm, so exactly one layer is supported"
    assert D == D_MODEL and S % SEQ_TILE == 0 and (B * S) % FFN_TILE == 0
    x1, w_gate, w_up, w_down = _mixer(
        x, norm1_g[0], w_in[0], ret_gn_g[0], gmlp_ln_g[0], gmlp_ln_b[0], w_s[0], b_s[0], w_out[0],
        w_ffn_gate[0], w_ffn_up[0], w_ffn_down[0])
    out = _ffn(x1.reshape(B * S, D), norm2_g[0], w_gate, w_up, w_down, final_g)
    return out.reshape(B, S, D)
```
